```python
import math
import jax, jax.numpy as jnp
from jax import lax
import numpy as np

D_MODEL = 1024
BATCH = 16
SEQ = 4096
DEPTH = 4

N_EVEN = (DEPTH + 1) // 2
N_ODD = DEPTH // 2

DN_ALPHA = (2.0 * DEPTH) ** 0.25
DN_BETA = (8.0 * DEPTH) ** -0.25
LN_EPS = 1e-5

MIX_WIDTH_A = D_MODEL // 2
MIX_WIDTH_B = D_MODEL // 2
CHUNK = 128
GM_GROUPS = 4
GM_GROUP_DIM = MIX_WIDTH_A // GM_GROUPS
DA_HEADS = 4
DA_HEAD_DIM = MIX_WIDTH_B // DA_HEADS // 2
DA_V_DIM = 2 * DA_HEAD_DIM
Q_BLOCK = 128
EVEN_IN = 2 * MIX_WIDTH_A + 3 * MIX_WIDTH_B

RET_HEADS = 4
RET_QK_DIM = D_MODEL // RET_HEADS
RET_V_DIM = 2 * D_MODEL // RET_HEADS
RET_CHUNK = 128
ODD_IN = 2 * D_MODEL + 4 * D_MODEL

D_FF = 7 * D_MODEL // 2
N_EXPERTS = 8
TOP_K = 2

kernel_name = 'hybrid_gmlp_diffattn_retention_moe_deepnorm'

F32 = jnp.float32


def layer_norm(x, g, b):
    xf = x.astype(F32)
    mu = jnp.mean(xf, axis=-1, keepdims=True)
    var = jnp.mean(jnp.square(xf - mu), axis=-1, keepdims=True)
    return ((xf - mu) * lax.rsqrt(var + LN_EPS) * g + b).astype(x.dtype)


def swiglu(x, w1, w3, w2):
    return (jax.nn.silu(x @ w1) * (x @ w3)) @ w2


def chunked_spatial_gating(u, v, w_s, b_s, ln_g, ln_b):
    bsz, s, _ = u.shape
    nc = s // CHUNK
    v = v.reshape(bsz, nc, CHUNK, GM_GROUPS, GM_GROUP_DIM)
    v = layer_norm(v, ln_g, ln_b)
    causal = jnp.tril(jnp.ones((CHUNK, CHUNK), dtype=bool))
    w = jnp.where(causal, w_s, jnp.zeros_like(w_s))
    mixed = jnp.einsum('gts,bcsgd->bctgd', w, v) + b_s.T[None, None, :, :, None]
    return u * mixed.reshape(bsz, s, MIX_WIDTH_A)


def diff_attention(q, k, v, lam_q1, lam_k1, lam_q2, lam_k2, subln_g, layer_idx):
    bsz, s = q.shape[0], q.shape[1]
    lam_init = 0.8 - 0.6 * math.exp(-0.3 * layer_idx)
    lam = (jnp.exp(jnp.sum(lam_q1.astype(F32) * lam_k1.astype(F32)))
           - jnp.exp(jnp.sum(lam_q2.astype(F32) * lam_k2.astype(F32))) + lam_init)
    slopes = 2.0 ** (-8.0 * jnp.arange(1, DA_HEADS + 1, dtype=F32) / DA_HEADS)
    scale = DA_HEAD_DIM ** -0.5
    nb = s // Q_BLOCK
    qb = q.reshape(bsz, nb, Q_BLOCK, DA_HEADS, 2, DA_HEAD_DIM).transpose(1, 0, 2, 3, 4, 5)
    kpos = jnp.arange(s)
    vf = v.astype(F32)

    def block(args):
        i, qi = args
        qpos = i * Q_BLOCK + jnp.arange(Q_BLOCK)
        dist = (qpos[:, None] - kpos[None, :]).astype(F32)
        bias = -slopes[:, None, None] * dist
        scores = jnp.einsum('bqhcd,bkhcd->cbhqk', qi, k, preferred_element_type=F32) * scale + bias
        scores = jnp.where(dist >= 0, scores, -jnp.inf)
        p = jax.nn.softmax(scores, axis=-1)
        attn = p[0] - lam * p[1]
        return jnp.einsum('bhqk,bkhe->bqhe', attn, vf)

    o = lax.map(block, (jnp.arange(nb), qb))
    o = o.transpose(1, 0, 2, 3, 4).reshape(bsz, s, DA_HEADS, DA_V_DIM)
    o = o * lax.rsqrt(jnp.mean(jnp.square(o), axis=-1, keepdims=True) + LN_EPS) * subln_g.astype(F32)
    o = o * (1.0 - lam_init)
    return o.reshape(bsz, s, MIX_WIDTH_B).astype(v.dtype)


def retention(q, k, v):
    bsz, s = q.shape[0], q.shape[1]
    nc = s // RET_CHUNK
    log_g = jnp.log1p(-(2.0 ** (-5.0 - jnp.arange(RET_HEADS, dtype=F32))))
    idx = jnp.arange(RET_CHUNK, dtype=F32)
    rel = idx[:, None] - idx[None, :]
    inner_decay = jnp.where(rel >= 0, jnp.exp(log_g[:, None, None] * jnp.maximum(rel, 0.0)), 0.0)
    q_decay = jnp.exp(log_g[None, :] * (idx[:, None] + 1.0))
    k_decay = jnp.exp(log_g[None, :] * (RET_CHUNK - 1.0 - idx[:, None]))
    chunk_decay = jnp.exp(log_g * RET_CHUNK)

    def to_chunks(t):
        return t.reshape(bsz, nc, RET_CHUNK, RET_HEADS, t.shape[-1]).transpose(1, 0, 2, 3, 4)

    def step(state, xs):
        qc, kc, vc = xs
        scores = jnp.einsum('bqhd,bkhd->bhqk', qc, kc) * inner_decay
        inner = jnp.einsum('bhqk,bkhe->bqhe', scores, vc)
        cross = jnp.einsum('bqhd,bhde->bqhe', qc, state) * q_decay[None, :, :, None]
        new_state = (state * chunk_decay[None, :, None, None]
                     + jnp.einsum('bkhd,bkhe->bhde', kc * k_decay[None, :, :, None], vc))
        return new_state, inner + cross

    state0 = jnp.zeros((bsz, RET_HEADS, RET_QK_DIM, RET_V_DIM), F32)
    _, o = lax.scan(step, state0, (to_chunks(q), to_chunks(k), to_chunks(v)))
    return o.transpose(1, 0, 2, 3, 4).reshape(bsz, s, RET_HEADS, RET_V_DIM)


def even_mixer(x, w_in, w_s, b_s, v_ln_g, v_ln_b, lam_q1, lam_k1, lam_q2, lam_k2, subln_g, w_o, layer_idx):
    bsz, s, _ = x.shape
    h = x @ w_in
    a = MIX_WIDTH_A
    bw = MIX_WIDTH_B
    u, v, q, k, va = jnp.split(h, [a, 2 * a, 2 * a + bw, 2 * a + 2 * bw], axis=-1)
    a_out = chunked_spatial_gating(jax.nn.gelu(u, approximate=False), jax.nn.gelu(v, approximate=False),
                                   w_s, b_s, v_ln_g, v_ln_b)
    q = q.reshape(bsz, s, DA_HEADS, 2, DA_HEAD_DIM)
    k = k.reshape(bsz, s, DA_HEADS, 2, DA_HEAD_DIM)
    va = va.reshape(bsz, s, DA_HEADS, DA_V_DIM)
    b_out = diff_attention(q, k, va, lam_q1, lam_k1, lam_q2, lam_k2, subln_g, layer_idx)
    return jnp.concatenate([a_out, b_out], axis=-1) @ w_o


def odd_mixer(x, w_in, gn_g, w_o):
    bsz, s, _ = x.shape
    h = x @ w_in
    q, k, v, g = jnp.split(h, [D_MODEL, 2 * D_MODEL, 4 * D_MODEL], axis=-1)
    q = q.reshape(bsz, s, RET_HEADS, RET_QK_DIM).astype(F32)
    k = k.reshape(bsz, s, RET_HEADS, RET_QK_DIM).astype(F32) * (RET_QK_DIM ** -0.5)
    v = v.reshape(bsz, s, RET_HEADS, RET_V_DIM).astype(F32)
    o = retention(q, k, v)
    mu = jnp.mean(o, axis=-1, keepdims=True)
    var = jnp.mean(jnp.square(o - mu), axis=-1, keepdims=True)
    o = (o - mu) * lax.rsqrt(var + LN_EPS) * gn_g.astype(F32)
    o = jax.nn.silu(g) * o.reshape(bsz, s, 2 * D_MODEL).astype(x.dtype)
    return o @ w_o


def moe_swiglu(x, w_router, b_router, w1, w3, w2):
    bsz, s, d = x.shape
    xt = x.reshape(-1, d)
    logits = (xt @ w_router).astype(F32) + b_router.astype(F32)
    top_val, top_idx = lax.top_k(logits, TOP_K)
    top_w = jax.nn.softmax(top_val, axis=-1)
    gates = jnp.sum(jax.nn.one_hot(top_idx, N_EXPERTS, dtype=F32) * top_w[..., None], axis=1)
    y = jnp.zeros(xt.shape, F32)
    for e in range(N_EXPERTS):
        y = y + gates[:, e:e + 1] * swiglu(xt, w1[e], w3[e], w2[e])
    return y.astype(x.dtype).reshape(bsz, s, d)


def setup_inputs(seed: int = 0) -> dict:
    key = jax.random.key(seed)
    ks = jax.random.split(key, 32)
    d = D_MODEL

    def nrm(k, shape, scale):
        return jax.random.normal(k, shape, F32) * scale

    def gain(k, shape):
        return 1.0 + 0.05 * jax.random.normal(k, shape, F32)

    return {
        'x': jax.random.normal(ks[0], (BATCH, SEQ, d), F32),
        'even_w_in': nrm(ks[1], (N_EVEN, d, EVEN_IN), d ** -0.5),
        'even_w_s': nrm(ks[2], (N_EVEN, GM_GROUPS, CHUNK, CHUNK), 0.5 * CHUNK ** -0.5),
        'even_b_s': gain(ks[3], (N_EVEN, GM_GROUPS, CHUNK)),
        'even_v_ln_g': gain(ks[4], (N_EVEN, GM_GROUPS, GM_GROUP_DIM)),
        'even_v_ln_b': nrm(ks[5], (N_EVEN, GM_GROUPS, GM_GROUP_DIM), 0.02),
        'even_lam_q1': nrm(ks[6], (N_EVEN, DA_HEAD_DIM), 0.1),
        'even_lam_k1': nrm(ks[7], (N_EVEN, DA_HEAD_DIM), 0.1),
        'even_lam_q2': nrm(ks[8], (N_EVEN, DA_HEAD_DIM), 0.1),
        'even_lam_k2': nrm(ks[9], (N_EVEN, DA_HEAD_DIM), 0.1),
        'even_subln_g': gain(ks[10], (N_EVEN, DA_V_DIM)),
        'even_w_o': nrm(ks[11], (N_EVEN, d, d), DN_BETA * d ** -0.5),
        'even_ln1_g': gain(ks[12], (N_EVEN, d)),
        'even_ln1_b': nrm(ks[13], (N_EVEN, d), 0.02),
        'ffn_w1': nrm(ks[14], (N_EVEN, d, D_FF), d ** -0.5),
        'ffn_w3': nrm(ks[15], (N_EVEN, d, D_FF), d ** -0.5),
        'ffn_w2': nrm(ks[16], (N_EVEN, D_FF, d), DN_BETA * D_FF ** -0.5),
        'even_ln2_g': gain(ks[17], (N_EVEN, d)),
        'even_ln2_b': nrm(ks[18], (N_EVEN, d), 0.02),
        'odd_w_in': nrm(ks[19], (N_ODD, d, ODD_IN), d ** -0.5),
        'odd_gn_g': gain(ks[20], (N_ODD, RET_HEADS, RET_V_DIM)),
        'odd_w_o': nrm(ks[21], (N_ODD, 2 * d, d), DN_BETA * (2 * d) ** -0.5),
        'odd_ln1_g': gain(ks[22], (N_ODD, d)),
        'odd_ln1_b': nrm(ks[23], (N_ODD, d), 0.02),
        'router_w': nrm(ks[24], (N_ODD, d, N_EXPERTS), d ** -0.5),
        'router_b': nrm(ks[25], (N_ODD, N_EXPERTS), 0.01),
        'moe_w1': nrm(ks[26], (N_ODD, N_EXPERTS, d, D_FF), d ** -0.5),
        'moe_w3': nrm(ks[27], (N_ODD, N_EXPERTS, d, D_FF), d ** -0.5),
        'moe_w2': nrm(ks[28], (N_ODD, N_EXPERTS, D_FF, d), DN_BETA * D_FF ** -0.5),
        'odd_ln2_g': gain(ks[29], (N_ODD, d)),
        'odd_ln2_b': nrm(ks[30], (N_ODD, d), 0.02),
    }


def reference(x, even_w_in, even_w_s, even_b_s, even_v_ln_g, even_v_ln_b, even_lam_q1, even_lam_k1,
              even_lam_q2, even_lam_k2, even_subln_g, even_w_o, even_ln1_g, even_ln1_b,
              ffn_w1, ffn_w3, ffn_w2, even_ln2_g, even_ln2_b,
              odd_w_in, odd_gn_g, odd_w_o, odd_ln1_g, odd_ln1_b,
              router_w, router_b, moe_w1, moe_w3, moe_w2, odd_ln2_g, odd_ln2_b):
    for l in range(DEPTH):
        i = l // 2
        if l % 2 == 0:
            mix = even_mixer(x, even_w_in[i], even_w_s[i], even_b_s[i], even_v_ln_g[i], even_v_ln_b[i],
                             even_lam_q1[i], even_lam_k1[i], even_lam_q2[i], even_lam_k2[i],
                             even_subln_g[i], even_w_o[i], l)
            x = layer_norm(DN_ALPHA * x + mix, even_ln1_g[i], even_ln1_b[i])
            x = layer_norm(DN_ALPHA * x + swiglu(x, ffn_w1[i], ffn_w3[i], ffn_w2[i]),
                           even_ln2_g[i], even_ln2_b[i])
        else:
            mix = odd_mixer(x, odd_w_in[i], odd_gn_g[i], odd_w_o[i])
            x = layer_norm(DN_ALPHA * x + mix, odd_ln1_g[i], odd_ln1_b[i])
            x = layer_norm(DN_ALPHA * x + moe_swiglu(x, router_w[i], router_b[i], moe_w1[i], moe_w3[i], moe_w2[i]),
                           odd_ln2_g[i], odd_ln2_b[i])
    return x
```

```python
import functools
import math

import jax
import jax.numpy as jnp
from jax import lax
from jax.experimental import pallas as pl
from jax.experimental.pallas import tpu as pltpu

F32 = jnp.float32
BF16 = jnp.bfloat16
I32 = jnp.int32

D_MODEL = 1024
DEPTH = 4
DN_ALPHA = (2.0 * DEPTH) ** 0.25
LN_EPS = 1e-5

MIX_A = D_MODEL // 2
GM_GROUPS = 4
GM_DIM = MIX_A // GM_GROUPS
GM_CHUNK = 128
DA_HEADS = 4
DA_HEAD_DIM = 64
DA_V_DIM = 128
RET_HEADS = 4
RET_DK = D_MODEL // RET_HEADS
RET_DV = 2 * D_MODEL // RET_HEADS
RET_CHUNK = 128
D_FF = 7 * D_MODEL // 2
N_EXPERTS = 8
TOP_K = 2

V7X_VMEM_LIMIT = 56 * 1024 * 1024
NEG_BIG = -1e30

NT_DIMS = (((1,), (1,)), ((), ()))
TN_DIMS = (((0,), (0,)), ((), ()))


def _cparams(sem):
    return pltpu.CompilerParams(dimension_semantics=sem, vmem_limit_bytes=V7X_VMEM_LIMIT)


def _layer_norm(z, g, b):
    mu = jnp.mean(z, axis=-1, keepdims=True)
    zc = z - mu
    var = jnp.mean(zc * zc, axis=-1, keepdims=True)
    return zc * lax.rsqrt(var + LN_EPS) * g + b


def _gelu(x):
    return 0.5 * x * (1.0 + lax.erf(x * (2.0 ** -0.5)))


def _silu(x):
    return x * jax.nn.sigmoid(x)


def _matmul_kernel(x_ref, w_ref, o_ref, *, col_chunk):
    xb = x_ref[...].astype(BF16)
    for c0 in range(0, o_ref.shape[1], col_chunk):
        o_ref[:, c0:c0 + col_chunk] = jnp.dot(
            xb, w_ref[:, c0:c0 + col_chunk], preferred_element_type=F32).astype(o_ref.dtype)


def _matmul(x, w, *, tm, tn):
    m, k = x.shape
    n = w.shape[1]
    return pl.pallas_call(
        functools.partial(_matmul_kernel, col_chunk=512),
        grid=(n // tn, m // tm),
        in_specs=[pl.BlockSpec((tm, k), lambda j, i: (i, 0)),
                  pl.BlockSpec((k, tn), lambda j, i: (0, j))],
        out_specs=pl.BlockSpec((tm, tn), lambda j, i: (i, j)),
        out_shape=jax.ShapeDtypeStruct((m, n), BF16),
        compiler_params=_cparams(("parallel", "parallel")),
        name="in_proj",
    )(x, w)


def _proj_res_ln_kernel(*refs, n_lhs):
    a_refs = refs[:n_lhs]
    w_ref, r_ref, g_ref, b_ref, o_ref = refs[n_lhs:]
    y = None
    k0 = 0
    for a_ref in a_refs:
        kk = a_ref.shape[1]
        part = jnp.dot(a_ref[...], w_ref[k0:k0 + kk, :], preferred_element_type=F32)
        y = part if y is None else y + part
        k0 += kk
    o_ref[...] = _layer_norm(DN_ALPHA * r_ref[...] + y, g_ref[...], b_ref[...])


def _proj_res_ln(a_list, w, res, g, b, *, tm):
    m, d = res.shape
    in_specs = [pl.BlockSpec((tm, a.shape[1]), lambda i: (i, 0)) for a in a_list]
    in_specs += [pl.BlockSpec(w.shape, lambda i: (0, 0), pipeline_mode=pl.Buffered(1)),
                 pl.BlockSpec((tm, d), lambda i: (i, 0)),
                 pl.BlockSpec((1, d), lambda i: (0, 0)),
                 pl.BlockSpec((1, d), lambda i: (0, 0))]
    return pl.pallas_call(
        functools.partial(_proj_res_ln_kernel, n_lhs=len(a_list)),
        grid=(m // tm,),
        in_specs=in_specs,
        out_specs=pl.BlockSpec((tm, d), lambda i: (i, 0)),
        out_shape=jax.ShapeDtypeStruct((m, d), F32),
        compiler_params=_cparams(("parallel",)),
        name="out_proj_ln",
    )(*a_list, w, res, g.reshape(1, d), b.reshape(1, d))


def _sgu_kernel(u_ref, v_ref, ws_ref, bst_ref, lg_ref, lb_ref, o_ref):
    tm = u_ref.shape[0]
    row = lax.broadcasted_iota(I32, (GM_CHUNK, GM_CHUNK), 0)
    col = lax.broadcasted_iota(I32, (GM_CHUNK, GM_CHUNK), 1)
    causal = row >= col
    for g in range(GM_GROUPS):
        w = jnp.where(causal, ws_ref[g], 0.0).astype(BF16)
        bias = bst_ref[:, g:g + 1]
        cols = slice(g * GM_DIM, (g + 1) * GM_DIM)
        for c0 in range(0, tm, GM_CHUNK):
            rows = slice(c0, c0 + GM_CHUNK)
            v = _gelu(v_ref[rows, cols].astype(F32))
            vn = _layer_norm(v, lg_ref[g:g + 1, :], lb_ref[g:g + 1, :])
            mixed = jnp.dot(w, vn.astype(BF16), preferred_element_type=F32) + bias
            u = _gelu(u_ref[rows, cols].astype(F32))
            o_ref[rows, cols] = (u * mixed).astype(o_ref.dtype)


def _sgu(h, w_s, b_s, ln_g, ln_b, *, tm):
    m = h.shape[0]
    return pl.pallas_call(
        _sgu_kernel,
        grid=(m // tm,),
        in_specs=[pl.BlockSpec((tm, MIX_A), lambda i: (i, 0)),
                  pl.BlockSpec((tm, MIX_A), lambda i: (i, 1)),
                  pl.BlockSpec(w_s.shape, lambda i: (0, 0, 0)),
                  pl.BlockSpec((GM_CHUNK, GM_GROUPS), lambda i: (0, 0)),
                  pl.BlockSpec(ln_g.shape, lambda i: (0, 0)),
                  pl.BlockSpec(ln_b.shape, lambda i: (0, 0))],
        out_specs=pl.BlockSpec((tm, MIX_A), lambda i: (i, 0)),
        out_shape=jax.ShapeDtypeStruct((m, MIX_A), BF16),
        compiler_params=_cparams(("parallel",)),
        name="sgu",
    )(h, h, w_s, b_s.T, ln_g, ln_b)


def _diff_attn_kernel(slope_ref, lam_ref, q_ref, k_ref, v_ref, g_ref, o_ref, *, tq, lam_init):
    h = pl.program_id(1)
    qi = pl.program_id(2)
    slope = slope_ref[h]
    lam = lam_ref[0]

    q = q_ref[0] * (DA_HEAD_DIM ** -0.5)
    lane = lax.broadcasted_iota(I32, q.shape, 1)
    zero = jnp.zeros_like(q)
    q_maps = (jnp.where(lane < DA_HEAD_DIM, q, zero), jnp.where(lane >= DA_HEAD_DIM, q, zero))

    row = lax.broadcasted_iota(I32, (tq, tq), 0)
    col = lax.broadcasted_iota(I32, (tq, tq), 1)
    bias = -slope * (row - col).astype(F32)

    def block(j, carry, diagonal):
        start = pl.multiple_of(j * tq, tq)
        kj = k_ref[0, pl.ds(start, tq), :]
        vj = v_ref[0, pl.ds(start, tq), :]
        off = -slope * ((qi - j) * tq).astype(F32)
        out = []
        for c in range(2):
            m, l, acc = carry[3 * c:3 * c + 3]
            s = lax.dot_general(q_maps[c], kj, NT_DIMS, preferred_element_type=F32) + bias
            if diagonal:
                s = jnp.where(row >= col, s, NEG_BIG)
            m_new = jnp.maximum(m, jnp.max(s, axis=-1, keepdims=True) + off)
            alpha = jnp.exp(m - m_new)
            p = jnp.exp(s - (m_new - off))
            l = alpha * l + jnp.sum(p, axis=-1, keepdims=True)
            acc = alpha * acc + jnp.dot(p.astype(BF16), vj, preferred_element_type=F32)
            out += [m_new, l, acc]
        return tuple(out)

    m0 = jnp.full((tq, 1), NEG_BIG, F32)
    l0 = jnp.zeros((tq, 1), F32)
    a0 = jnp.zeros((tq, DA_V_DIM), F32)
    carry = lax.fori_loop(0, qi, lambda j, c: block(j, c, False), (m0, l0, a0, m0, l0, a0))
    _, l_a, acc_a, _, l_b, acc_b = block(qi, carry, True)

    o = acc_a / l_a - lam * (acc_b / l_b)
    o = o * lax.rsqrt(jnp.mean(o * o, axis=-1, keepdims=True) + LN_EPS) * g_ref[...]
    o_ref[0] = (o * (1.0 - lam_init)).astype(o_ref.dtype)


def _diff_attn(h3, slopes, lam, subln_g, *, tq, lam_init):
    bsz, s, _ = h3.shape
    blk = DA_V_DIM
    q_blk0 = 2 * MIX_A // blk
    k_blk0 = q_blk0 + DA_HEADS
    v_blk0 = k_blk0 + DA_HEADS
    smem = pl.BlockSpec(memory_space=pltpu.SMEM)
    return pl.pallas_call(
        functools.partial(_diff_attn_kernel, tq=tq, lam_init=lam_init),
        grid=(bsz, DA_HEADS, s // tq),
        in_specs=[smem, smem,
                  pl.BlockSpec((1, tq, blk), lambda b, h, i: (b, i, q_blk0 + h)),
                  pl.BlockSpec((1, s, blk), lambda b, h, i: (b, 0, k_blk0 + h)),
                  pl.BlockSpec((1, s, blk), lambda b, h, i: (b, 0, v_blk0 + h)),
                  pl.BlockSpec((1, blk), lambda b, h, i: (0, 0))],
        out_specs=pl.BlockSpec((1, tq, blk), lambda b, h, i: (b, i, h)),
        out_shape=jax.ShapeDtypeStruct((bsz, s, DA_HEADS * DA_V_DIM), BF16),
        compiler_params=_cparams(("parallel", "parallel", "parallel")),
        name="diff_attn",
    )(slopes, lam, h3, h3, h3, subln_g.reshape(1, blk))


def _retention_kernel(q_ref, k_ref, v_ref, g_ref, gn_ref, o_ref, state_ref):
    tr = q_ref.shape[1]

    @pl.when(pl.program_id(1) == 0)
    def _():
        state_ref[...] = jnp.zeros_like(state_ref)

    idx_r = lax.broadcasted_iota(I32, (RET_CHUNK, RET_CHUNK), 0)
    idx_c = lax.broadcasted_iota(I32, (RET_CHUNK, RET_CHUNK), 1)
    rel = (idx_r - idx_c).astype(F32)
    pos = lax.broadcasted_iota(I32, (RET_CHUNK, 1), 0).astype(F32)
    for h in range(RET_HEADS):
        log_g = math.log1p(-(2.0 ** (-5.0 - h)))
        inner_decay = jnp.where(rel >= 0, jnp.exp(log_g * jnp.maximum(rel, 0.0)), 0.0)
        q_decay = jnp.exp(log_g * (pos + 1.0))
        k_decay = jnp.exp(log_g * (RET_CHUNK - 1.0 - pos))
        chunk_decay = math.exp(log_g * RET_CHUNK)
        kc = slice(h * RET_DK, (h + 1) * RET_DK)
        vc = slice(h * RET_DV, (h + 1) * RET_DV)
        for c0 in range(0, tr, RET_CHUNK):
            rows = slice(c0, c0 + RET_CHUNK)
            q = q_ref[0, rows, kc]
            k = k_ref[0, rows, kc] * (RET_DK ** -0.5)
            v = v_ref[0, rows, vc]
            state = state_ref[h]
            scores = lax.dot_general(q, k, NT_DIMS, preferred_element_type=F32) * inner_decay
            inner = jnp.dot(scores.astype(BF16), v, preferred_element_type=F32)
            cross = jnp.dot(q, state.astype(BF16), preferred_element_type=F32) * q_decay
            k_dec = (k.astype(F32) * k_decay).astype(BF16)
            state_ref[h] = state * chunk_decay + lax.dot_general(
                k_dec, v, TN_DIMS, preferred_element_type=F32)
            o = inner + cross
            mu = jnp.mean(o, axis=-1, keepdims=True)
            oc = o - mu
            var = jnp.mean(oc * oc, axis=-1, keepdims=True)
            on = oc * lax.rsqrt(var + LN_EPS) * gn_ref[h:h + 1, :]
            gate = g_ref[0, rows, vc].astype(F32)
            o_ref[0, rows, vc] = (_silu(gate) * on).astype(o_ref.dtype)


def _retention(h3, gn_g, *, tr):
    bsz, s, _ = h3.shape
    dq = RET_HEADS * RET_DK
    dv = RET_HEADS * RET_DV
    return pl.pallas_call(
        _retention_kernel,
        grid=(bsz, s // tr),
        in_specs=[pl.BlockSpec((1, tr, dq), lambda b, i: (b, i, 0)),
                  pl.BlockSpec((1, tr, dq), lambda b, i: (b, i, 1)),
                  pl.BlockSpec((1, tr, dv), lambda b, i: (b, i, 1)),
                  pl.BlockSpec((1, tr, dv), lambda b, i: (b, i, 2)),
                  pl.BlockSpec(gn_g.shape, lambda b, i: (0, 0))],
        out_specs=pl.BlockSpec((1, tr, dv), lambda b, i: (b, i, 0)),
        out_shape=jax.ShapeDtypeStruct((bsz, s, dv), BF16),
        scratch_shapes=[pltpu.VMEM((RET_HEADS, RET_DK, RET_DV), F32)],
        compiler_params=_cparams(("parallel", "arbitrary")),
        name="retention",
    )(h3, h3, h3, h3, gn_g)


def _swiglu_partial(xb, w1, w3, w2):
    h1 = jnp.dot(xb, w1, preferred_element_type=F32)
    h3 = jnp.dot(xb, w3, preferred_element_type=F32)
    return jnp.dot((_silu(h1) * h3).astype(BF16), w2, preferred_element_type=F32)


def _ffn_kernel(x_ref, w1_ref, w3_ref, w2_ref, g_ref, b_ref, o_ref, acc_ref, xb_ref):
    f = pl.program_id(1)

    @pl.when(f == 0)
    def _():
        xb_ref[...] = x_ref[...].astype(BF16)

    part = _swiglu_partial(xb_ref[...], w1_ref[...], w3_ref[...], w2_ref[...])

    @pl.when(f == 0)
    def _():
        acc_ref[...] = part

    @pl.when(f > 0)
    def _():
        acc_ref[...] += part

    @pl.when(f == pl.num_programs(1) - 1)
    def _():
        o_ref[...] = _layer_norm(DN_ALPHA * x_ref[...] + acc_ref[...], g_ref[...], b_ref[...])


def _ffn(x, w1, w3, w2, g, b, *, tm, tf):
    m, d = x.shape
    return pl.pallas_call(
        _ffn_kernel,
        grid=(m // tm, D_FF // tf),
        in_specs=[pl.BlockSpec((tm, d), lambda i, f: (i, 0)),
                  pl.BlockSpec((d, tf), lambda i, f: (0, f)),
                  pl.BlockSpec((d, tf), lambda i, f: (0, f)),
                  pl.BlockSpec((tf, d), lambda i, f: (f, 0)),
                  pl.BlockSpec((1, d), lambda i, f: (0, 0)),
                  pl.BlockSpec((1, d), lambda i, f: (0, 0))],
        out_specs=pl.BlockSpec((tm, d), lambda i, f: (i, 0)),
        out_shape=jax.ShapeDtypeStruct((m, d), F32),
        scratch_shapes=[pltpu.VMEM((tm, d), F32), pltpu.VMEM((tm, d), BF16)],
        compiler_params=_cparams(("parallel", "arbitrary")),
        name="ffn",
    )(x, w1, w3, w2, g.reshape(1, d), b.reshape(1, d))


def _experts_kernel(te_ref, nv_ref, xs_ref, w1_ref, w3_ref, w2_ref, ys_ref, acc_ref, xb_ref):
    del te_ref
    i = pl.program_id(0)
    f = pl.program_id(1)

    @pl.when(i < nv_ref[0])
    def _():
        @pl.when(f == 0)
        def _():
            xb_ref[...] = xs_ref[...].astype(BF16)

        part = _swiglu_partial(xb_ref[...], w1_ref[0], w3_ref[0], w2_ref[0])

        @pl.when(f == 0)
        def _():
            acc_ref[...] = part

        @pl.when(f > 0)
        def _():
            acc_ref[...] += part

        @pl.when(f == pl.num_programs(1) - 1)
        def _():
            ys_ref[...] = acc_ref[...]

    @pl.when(i >= nv_ref[0])
    def _():
        ys_ref[...] = jnp.zeros_like(ys_ref)


def _experts(xs, tile_expert, n_valid, w1, w3, w2, *, tm, tf):
    r, d = xs.shape
    nf = D_FF // tf

    def row_tile(i, f, te, nv):
        return jnp.minimum(i, nv[0] - 1)

    def f_tile(i, f, te, nv):
        return jnp.where(i < nv[0], f, nf - 1)

    grid_spec = pltpu.PrefetchScalarGridSpec(
        num_scalar_prefetch=2,
        grid=(r // tm, nf),
        in_specs=[pl.BlockSpec((tm, d), lambda i, f, te, nv: (row_tile(i, f, te, nv), 0)),
                  pl.BlockSpec((1, d, tf), lambda i, f, te, nv: (te[i], 0, f_tile(i, f, te, nv))),
                  pl.BlockSpec((1, d, tf), lambda i, f, te, nv: (te[i], 0, f_tile(i, f, te, nv))),
                  pl.BlockSpec((1, tf, d), lambda i, f, te, nv: (te[i], f_tile(i, f, te, nv), 0))],
        out_specs=pl.BlockSpec((tm, d), lambda i, f, te, nv: (i, 0)),
        scratch_shapes=[pltpu.VMEM((tm, d), F32), pltpu.VMEM((tm, d), BF16)],
    )
    return pl.pallas_call(
        _experts_kernel,
        grid_spec=grid_spec,
        out_shape=jax.ShapeDtypeStruct((r, d), F32),
        compiler_params=_cparams(("arbitrary", "arbitrary")),
        name="experts",
    )(tile_expert, n_valid, xs, w1, w3, w2)


def _split_bf16(x):
    hi = x.astype(BF16)
    return hi, (x - hi.astype(F32)).astype(BF16)


def _router_kernel(x_ref, wt_ref, b_ref, meta_ref, gate_ref, cnt_ref, carry_ref, tri_ref):
    i = pl.program_id(0)
    tm = x_ref.shape[0]

    @pl.when(i == 0)
    def _():
        carry_ref[...] = jnp.zeros_like(carry_ref)
        a = lax.broadcasted_iota(I32, (tm, tm), 0)
        b = lax.broadcasted_iota(I32, (tm, tm), 1)
        tri_ref[...] = (a <= b).astype(BF16)

    xh, xl = _split_bf16(x_ref[...])
    wh, wl = _split_bf16(wt_ref[...])
    logits = (lax.dot_general(wh, xh, NT_DIMS, preferred_element_type=F32)
              + lax.dot_general(wh, xl, NT_DIMS, preferred_element_type=F32)
              + lax.dot_general(wl, xh, NT_DIMS, preferred_element_type=F32)
              + b_ref[...])

    e_iota = lax.broadcasted_iota(I32, logits.shape, 0)
    m1 = jnp.max(logits, axis=0, keepdims=True)
    i1 = jnp.min(jnp.where(logits == m1, e_iota, N_EXPERTS), axis=0, keepdims=True)
    rest = jnp.where(e_iota == i1, -jnp.inf, logits)
    m2 = jnp.max(rest, axis=0, keepdims=True)
    i2 = jnp.min(jnp.where(rest == m2, e_iota, N_EXPERTS), axis=0, keepdims=True)
    e2 = jnp.exp(m2 - m1)
    gate_ref[0:1, :] = 1.0 / (1.0 + e2)
    gate_ref[1:2, :] = e2 / (1.0 + e2)

    pick1 = e_iota == i1
    pick2 = e_iota == i2
    chosen = jnp.where(pick1 | pick2, 1.0, 0.0)
    incl = jnp.dot(chosen.astype(BF16), tri_ref[...], preferred_element_type=F32)
    before = carry_ref[...] + incl - chosen
    meta_ref[0:1, :] = i1
    meta_ref[1:2, :] = i2
    meta_ref[2:3, :] = jnp.sum(jnp.where(pick1, before, 0.0), axis=0, keepdims=True).astype(I32)
    meta_ref[3:4, :] = jnp.sum(jnp.where(pick2, before, 0.0), axis=0, keepdims=True).astype(I32)
    carry_ref[...] += incl[:, tm - 1:tm]
    cnt_ref[...] = jnp.broadcast_to(carry_ref[...], cnt_ref.shape).astype(I32)


def _router(x, w_router, b_router, *, tm):
    m, d = x.shape
    return pl.pallas_call(
        _router_kernel,
        grid=(m // tm,),
        in_specs=[pl.BlockSpec((tm, d), lambda i: (i, 0)),
                  pl.BlockSpec((N_EXPERTS, d), lambda i: (0, 0)),
                  pl.BlockSpec((N_EXPERTS, 1), lambda i: (0, 0))],
        out_specs=[pl.BlockSpec((4, tm), lambda i: (0, i)),
                   pl.BlockSpec((TOP_K, tm), lambda i: (0, i)),
                   pl.BlockSpec((N_EXPERTS, 128), lambda i: (0, 0))],
        out_shape=[jax.ShapeDtypeStruct((4, m), I32),
                   jax.ShapeDtypeStruct((TOP_K, m), F32),
                   jax.ShapeDtypeStruct((N_EXPERTS, 128), I32)],
        scratch_shapes=[pltpu.VMEM((N_EXPERTS, 1), F32), pltpu.VMEM((tm, tm), BF16)],
        compiler_params=_cparams(("arbitrary",)),
        name="router",
    )(x, w_router.T, b_router.reshape(N_EXPERTS, 1))


def _row_copies(src_of, dst_of, sem, n):
    def start(r, carry):
        pltpu.make_async_copy(src_of(r), dst_of(r), sem).start()
        return carry

    def wait(r, carry):
        pltpu.make_async_copy(src_of(r), dst_of(r), sem).wait()
        return carry

    lax.fori_loop(0, n, start, 0, unroll=8)
    lax.fori_loop(0, n, wait, 0, unroll=8)


def _dispatch_kernel(dest_ref, x_ref, xs_in_ref, xs_ref, sem):
    del xs_in_ref
    tt = x_ref.shape[0]
    for j in range(TOP_K):
        _row_copies(lambda r: x_ref.at[pl.ds(r, 1), :],
                    lambda r, j=j: xs_ref.at[pl.ds(dest_ref[0, 0, j * tt + r], 1), :],
                    sem.at[j], tt)


def _dispatch(x, dest_tiles, rows, *, tt):
    m, d = x.shape
    xs0 = jnp.zeros((rows, d), F32)
    return pl.pallas_call(
        _dispatch_kernel,
        grid=(m // tt,),
        in_specs=[pl.BlockSpec((1, 1, TOP_K * tt), lambda i: (i, 0, 0), memory_space=pltpu.SMEM),
                  pl.BlockSpec((tt, d), lambda i: (i, 0)),
                  pl.BlockSpec(memory_space=pl.ANY)],
        out_specs=pl.BlockSpec(memory_space=pl.ANY),
        out_shape=jax.ShapeDtypeStruct((rows, d), F32),
        scratch_shapes=[pltpu.SemaphoreType.DMA((TOP_K,))],
        input_output_aliases={2: 0},
        compiler_params=_cparams(("arbitrary",)),
        name="dispatch",
    )(dest_tiles, x, xs0)


def _combine_kernel(dest_ref, x_ref, gate_ref, g_ref, b_ref, ys_ref, o_ref, buf_ref, sem):
    tt = x_ref.shape[0]
    for j in range(TOP_K):
        _row_copies(lambda r, j=j: ys_ref.at[pl.ds(dest_ref[0, 0, j * tt + r], 1), :],
                    lambda r, j=j: buf_ref.at[j, pl.ds(r, 1), :],
                    sem.at[j], tt)
    y = jnp.zeros(x_ref.shape, F32)
    for j in range(TOP_K):
        y = y + gate_ref[:, j:j + 1] * buf_ref[j]
    o_ref[...] = _layer_norm(DN_ALPHA * x_ref[...] + y, g_ref[...], b_ref[...])


def _combine(x, ys, dest_tiles, gates_t, g, b, *, tt):
    m, d = x.shape
    return pl.pallas_call(
        _combine_kernel,
        grid=(m // tt,),
        in_specs=[pl.BlockSpec((1, 1, TOP_K * tt), lambda i: (i, 0, 0), memory_space=pltpu.SMEM),
                  pl.BlockSpec((tt, d), lambda i: (i, 0)),
                  pl.BlockSpec((tt, TOP_K), lambda i: (i, 0)),
                  pl.BlockSpec((1, d), lambda i: (0, 0)),
                  pl.BlockSpec((1, d), lambda i: (0, 0)),
                  pl.BlockSpec(memory_space=pl.ANY)],
        out_specs=pl.BlockSpec((tt, d), lambda i: (i, 0)),
        out_shape=jax.ShapeDtypeStruct((m, d), F32),
        scratch_shapes=[pltpu.VMEM((TOP_K, tt, d), F32), pltpu.SemaphoreType.DMA((TOP_K,))],
        compiler_params=_cparams(("arbitrary",)),
        name="combine",
    )(dest_tiles, x, gates_t, g.reshape(1, d), b.reshape(1, d), ys)


def _moe(x, w_router, b_router, w1, w3, w2, g, b, *, tile):
    m, d = x.shape
    meta, gates, counts = _router(x, w_router, b_router, tm=tile.router)
    counts = counts[:, 0]
    te = tile.expert
    padded = (counts + te - 1) // te * te
    ends = jnp.cumsum(padded)
    dest = (ends - padded)[meta[0:TOP_K]] + meta[TOP_K:2 * TOP_K]
    n_tiles = TOP_K * m // te + N_EXPERTS
    tile_expert = jnp.minimum(
        jnp.searchsorted(ends, jnp.arange(n_tiles, dtype=I32) * te, side="right"), N_EXPERTS - 1).astype(I32)
    n_valid = (ends[-1:] // te).astype(I32)

    tt = tile.token
    dest_tiles = dest.reshape(TOP_K, m // tt, tt).transpose(1, 0, 2).reshape(m // tt, 1, TOP_K * tt)
    xs = _dispatch(x, dest_tiles, n_tiles * te, tt=tt)
    ys = _experts(xs, tile_expert, n_valid, w1, w3, w2, tm=te, tf=tile.ff)
    return _combine(x, ys, dest_tiles, gates.T, g, b, tt=tt)


class _Tiles:
    def __init__(self, bsz, seq):
        m = bsz * seq
        self.proj = min(512, m)
        self.sgu = min(512, seq)
        self.attn = min(512, seq)
        self.ret = min(512, seq)
        self.ffn = min(1024, m)
        self.ff = 512
        self.router = min(1024, m)
        self.expert = min(512, m)
        self.token = min(128, m)


def _even_layer(x, p, layer_idx, bsz, seq, tile):
    m, d = x.shape
    h = _matmul(x, p["w_in"], tm=tile.proj, tn=p["w_in"].shape[1])
    a_out = _sgu(h, p["w_s"], p["b_s"], p["v_ln_g"], p["v_ln_b"], tm=tile.sgu)
    lam_init = 0.8 - 0.6 * math.exp(-0.3 * layer_idx)
    lam = (jnp.exp(jnp.sum(p["lam_q1"] * p["lam_k1"])) - jnp.exp(jnp.sum(p["lam_q2"] * p["lam_k2"]))
           + lam_init).reshape(1).astype(F32)
    slopes = 2.0 ** (-8.0 * jnp.arange(1, DA_HEADS + 1, dtype=F32) / DA_HEADS)
    b_out = _diff_attn(h.reshape(bsz, seq, -1), slopes, lam, p["subln_g"], tq=tile.attn, lam_init=lam_init)
    x = _proj_res_ln([a_out, b_out.reshape(m, -1)], p["w_o"], x, p["ln1_g"], p["ln1_b"], tm=tile.proj)
    return _ffn(x, p["ffn_w1"], p["ffn_w3"], p["ffn_w2"], p["ln2_g"], p["ln2_b"], tm=tile.ffn, tf=tile.ff)


def _odd_layer(x, p, bsz, seq, tile):
    m, d = x.shape
    n_in = p["w_in"].shape[1]
    h = _matmul(x, p["w_in"], tm=tile.proj, tn=n_in // 2)
    o = _retention(h.reshape(bsz, seq, n_in), p["gn_g"], tr=tile.ret)
    x = _proj_res_ln([o.reshape(m, -1)], p["w_o"], x, p["ln1_g"], p["ln1_b"], tm=tile.proj)
    return _moe(x, p["router_w"], p["router_b"], p["moe_w1"], p["moe_w3"], p["moe_w2"],
                p["ln2_g"], p["ln2_b"], tile=tile)


def kernel(x, even_w_in, even_w_s, even_b_s, even_v_ln_g, even_v_ln_b, even_lam_q1, even_lam_k1, even_lam_q2, even_lam_k2, even_subln_g, even_w_o, even_ln1_g, even_ln1_b, ffn_w1, ffn_w3, ffn_w2, even_ln2_g, even_ln2_b, odd_w_in, odd_gn_g, odd_w_o, odd_ln1_g, odd_ln1_b, router_w, router_b, moe_w1, moe_w3, moe_w2, odd_ln2_g, odd_ln2_b):
    bsz, seq, d = x.shape
    tile = _Tiles(bsz, seq)
    bf = lambda w: w.astype(BF16)
    xf = x.reshape(bsz * seq, d)
    for layer in range(DEPTH):
        i = layer // 2
        if layer % 2 == 0:
            p = dict(w_in=bf(even_w_in[i]), w_s=even_w_s[i], b_s=even_b_s[i], v_ln_g=even_v_ln_g[i],
                     v_ln_b=even_v_ln_b[i], lam_q1=even_lam_q1[i], lam_k1=even_lam_k1[i],
                     lam_q2=even_lam_q2[i], lam_k2=even_lam_k2[i], subln_g=even_subln_g[i],
                     w_o=bf(even_w_o[i]), ln1_g=even_ln1_g[i], ln1_b=even_ln1_b[i],
                     ffn_w1=bf(ffn_w1[i]), ffn_w3=bf(ffn_w3[i]), ffn_w2=bf(ffn_w2[i]),
                     ln2_g=even_ln2_g[i], ln2_b=even_ln2_b[i])
            xf = _even_layer(xf, p, layer, bsz, seq, tile)
        else:
            p = dict(w_in=bf(odd_w_in[i]), gn_g=odd_gn_g[i], w_o=bf(odd_w_o[i]),
                     ln1_g=odd_ln1_g[i], ln1_b=odd_ln1_b[i], router_w=router_w[i], router_b=router_b[i],
                     moe_w1=bf(moe_w1[i]), moe_w3=bf(moe_w3[i]), moe_w2=bf(moe_w2[i]),
                     ln2_g=odd_ln2_g[i], ln2_b=odd_ln2_b[i])
            xf = _odd_layer(xf, p, bsz, seq, tile)
    return xf.reshape(bsz, seq, d)
```

```python
import functools
import math

import jax
import jax.numpy as jnp
from jax import lax
from jax.experimental import pallas as pl
from jax.experimental.pallas import tpu as pltpu

F32 = jnp.float32
BF16 = jnp.bfloat16
I32 = jnp.int32

D_MODEL = 1024
DEPTH = 4
DN_ALPHA = (2.0 * DEPTH) ** 0.25
LN_EPS = 1e-5

MIX_A = D_MODEL // 2
GM_GROUPS = 4
GM_DIM = MIX_A // GM_GROUPS
GM_CHUNK = 128
DA_HEADS = 4
DA_HEAD_DIM = 64
DA_V_DIM = 128
RET_HEADS = 4
RET_DK = D_MODEL // RET_HEADS
RET_DV = 2 * D_MODEL // RET_HEADS
RET_CHUNK = 128
D_FF = 7 * D_MODEL // 2
N_EXPERTS = 8
TOP_K = 2

V7X_VMEM_LIMIT = 56 * 1024 * 1024
NEG_BIG = -1e30
LOG2E = 1.4426950408889634

NT_DIMS = (((1,), (1,)), ((), ()))
TN_DIMS = (((0,), (0,)), ((), ()))


def _cparams(sem):
    return pltpu.CompilerParams(dimension_semantics=sem, vmem_limit_bytes=V7X_VMEM_LIMIT)


def _layer_norm(z, g, b):
    mu = jnp.mean(z, axis=-1, keepdims=True)
    zc = z - mu
    var = jnp.mean(zc * zc, axis=-1, keepdims=True)
    return zc * lax.rsqrt(var + LN_EPS) * g + b


def _gelu(x):
    return 0.5 * x * (1.0 + lax.erf(x * (2.0 ** -0.5)))


def _silu(x):
    return x * jax.nn.sigmoid(x)


def _matmul_kernel(x_ref, w_ref, o_ref, *, col_chunk, scaled_cols, scale):
    xb = x_ref[...].astype(BF16)
    for c0 in range(0, o_ref.shape[1], col_chunk):
        y = jnp.dot(xb, w_ref[:, c0:c0 + col_chunk], preferred_element_type=F32)
        if scaled_cols is not None and scaled_cols[0] <= c0 < scaled_cols[1]:
            y = y * scale
        o_ref[:, c0:c0 + col_chunk] = y.astype(o_ref.dtype)


def _matmul(x, w, *, tm, tn, scaled_cols=None, scale=1.0):
    m, k = x.shape
    n = w.shape[1]
    col_chunk = 512
    if scaled_cols is not None:
        assert tn == n and scaled_cols[0] % col_chunk == 0 and scaled_cols[1] % col_chunk == 0
    return pl.pallas_call(
        functools.partial(_matmul_kernel, col_chunk=col_chunk, scaled_cols=scaled_cols, scale=scale),
        grid=(n // tn, m // tm),
        in_specs=[pl.BlockSpec((tm, k), lambda j, i: (i, 0)),
                  pl.BlockSpec((k, tn), lambda j, i: (0, j))],
        out_specs=pl.BlockSpec((tm, tn), lambda j, i: (i, j)),
        out_shape=jax.ShapeDtypeStruct((m, n), BF16),
        compiler_params=_cparams(("parallel", "parallel")),
        name="in_proj",
    )(x, w)


def _proj_res_ln_kernel(*refs, n_lhs):
    a_refs = refs[:n_lhs]
    w_ref, r_ref, g_ref, b_ref, o_ref = refs[n_lhs:]
    y = None
    k0 = 0
    for a_ref in a_refs:
        kk = a_ref.shape[1]
        part = jnp.dot(a_ref[...], w_ref[k0:k0 + kk, :], preferred_element_type=F32)
        y = part if y is None else y + part
        k0 += kk
    o_ref[...] = _layer_norm(DN_ALPHA * r_ref[...] + y, g_ref[...], b_ref[...])


def _proj_res_ln(a_list, w, res, g, b, *, tm):
    m, d = res.shape
    in_specs = [pl.BlockSpec((tm, a.shape[1]), lambda i: (i, 0)) for a in a_list]
    in_specs += [pl.BlockSpec(w.shape, lambda i: (0, 0), pipeline_mode=pl.Buffered(1)),
                 pl.BlockSpec((tm, d), lambda i: (i, 0)),
                 pl.BlockSpec((1, d), lambda i: (0, 0)),
                 pl.BlockSpec((1, d), lambda i: (0, 0))]
    return pl.pallas_call(
        functools.partial(_proj_res_ln_kernel, n_lhs=len(a_list)),
        grid=(m // tm,),
        in_specs=in_specs,
        out_specs=pl.BlockSpec((tm, d), lambda i: (i, 0)),
        out_shape=jax.ShapeDtypeStruct((m, d), F32),
        compiler_params=_cparams(("parallel",)),
        name="out_proj_ln",
    )(*a_list, w, res, g.reshape(1, d), b.reshape(1, d))


def _sgu_kernel(u_ref, v_ref, ws_ref, bst_ref, lg_ref, lb_ref, o_ref):
    tm = u_ref.shape[0]
    row = lax.broadcasted_iota(I32, (GM_CHUNK, GM_CHUNK), 0)
    col = lax.broadcasted_iota(I32, (GM_CHUNK, GM_CHUNK), 1)
    causal = row >= col
    for g in range(GM_GROUPS):
        w = jnp.where(causal, ws_ref[g], 0.0).astype(BF16)
        bias = bst_ref[:, g:g + 1]
        cols = slice(g * GM_DIM, (g + 1) * GM_DIM)
        for c0 in range(0, tm, GM_CHUNK):
            rows = slice(c0, c0 + GM_CHUNK)
            v = _gelu(v_ref[rows, cols].astype(F32))
            vn = _layer_norm(v, lg_ref[g:g + 1, :], lb_ref[g:g + 1, :])
            mixed = jnp.dot(w, vn.astype(BF16), preferred_element_type=F32) + bias
            u = _gelu(u_ref[rows, cols].astype(F32))
            o_ref[rows, cols] = (u * mixed).astype(o_ref.dtype)


def _sgu(h, w_s, b_s, ln_g, ln_b, *, tm):
    m = h.shape[0]
    return pl.pallas_call(
        _sgu_kernel,
        grid=(m // tm,),
        in_specs=[pl.BlockSpec((tm, MIX_A), lambda i: (i, 0)),
                  pl.BlockSpec((tm, MIX_A), lambda i: (i, 1)),
                  pl.BlockSpec(w_s.shape, lambda i: (0, 0, 0)),
                  pl.BlockSpec((GM_CHUNK, GM_GROUPS), lambda i: (0, 0)),
                  pl.BlockSpec(ln_g.shape, lambda i: (0, 0)),
                  pl.BlockSpec(ln_b.shape, lambda i: (0, 0))],
        out_specs=pl.BlockSpec((tm, MIX_A), lambda i: (i, 0)),
        out_shape=jax.ShapeDtypeStruct((m, MIX_A), BF16),
        compiler_params=_cparams(("parallel",)),
        name="sgu",
    )(h, h, w_s, b_s.T, ln_g, ln_b)


def _alibi_features(slopes2, tq):
    rest = slopes2
    pieces = []
    for _ in range(3):
        piece = rest.astype(BF16).astype(F32)
        pieces.append(piece)
        rest = rest - piece
    pos = jnp.arange(tq, dtype=I32)
    hi = jnp.broadcast_to(((pos >> 4) << 4).astype(F32), (DA_HEADS, tq))
    lo = jnp.broadcast_to((pos & 15).astype(F32), (DA_HEADS, tq))
    qf = jnp.zeros((DA_HEADS, tq, DA_V_DIM), F32)
    kf = jnp.zeros((DA_HEADS, tq, DA_V_DIM), F32)
    for t, piece in enumerate(pieces):
        const = jnp.broadcast_to(piece[:, None], (DA_HEADS, tq))
        a, b = 2 * t, 2 * t + 1
        qf = qf.at[:, :, a].set(hi).at[:, :, b].set(lo)
        kf = kf.at[:, :, a].set(-const).at[:, :, b].set(-const)
        a, b = 6 + 2 * t, 7 + 2 * t
        qf = qf.at[:, :, a].set(const).at[:, :, b].set(const)
        kf = kf.at[:, :, a].set(hi).at[:, :, b].set(lo)
    return qf.astype(BF16), kf.astype(BF16)


def _diff_attn_kernel(sl_ref, lam_ref, q_ref, k_ref, v_ref, qf_ref, kf_ref, g_ref, o_ref,
                      sa_ref, sb_ref, acc_ref, m_ref, *, tq, lam_init):
    h = pl.program_id(1)
    qi = pl.program_id(2)
    slope2 = sl_ref[h]
    lam = lam_ref[0]

    lane = lax.broadcasted_iota(I32, (tq, DA_V_DIM), 1)
    ones_col = jnp.where(lane == 0, 1.0, 0.0).astype(BF16)
    q = q_ref[0]
    zero = jnp.zeros_like(q)
    q_maps = tuple(jnp.concatenate([jnp.where(keep, q, zero), qf_ref[0]], axis=1)
                   for keep in (lane < DA_HEAD_DIM, lane >= DA_HEAD_DIM))
    kf = kf_ref[0]

    row = lax.broadcasted_iota(I32, (tq, tq), 0)
    col = lax.broadcasted_iota(I32, (tq, tq), 1)

    def scores(j, s_ref):
        start = pl.multiple_of(j * tq, tq)
        kj = jnp.concatenate([k_ref[0, pl.ds(start, tq), :], kf], axis=1)
        for c in range(2):
            s_ref[c] = lax.dot_general(q_maps[c], kj, NT_DIMS, preferred_element_type=F32)

    def update(j, s_ref, diagonal):
        start = pl.multiple_of(j * tq, tq)
        vj = jnp.concatenate([v_ref[0, pl.ds(start, tq), :], ones_col], axis=1)
        off = -slope2 * ((qi - j) * tq).astype(F32)
        for c in range(2):
            s = s_ref[c]
            if diagonal:
                s = jnp.where(row >= col, s, NEG_BIG)
            m = m_ref[c]
            m_new = jnp.maximum(m, jnp.max(s, axis=-1, keepdims=True) + off)
            p = jnp.exp2(s - (m_new - off)).astype(BF16)
            acc_ref[c] = jnp.exp2(m - m_new) * acc_ref[c] + jnp.dot(p, vj, preferred_element_type=F32)
            m_ref[c] = m_new

    m_ref[...] = jnp.full(m_ref.shape, NEG_BIG, F32)
    acc_ref[...] = jnp.zeros_like(acc_ref)

    scores(0, sa_ref)

    def pair(t, carry):
        b = 2 * t
        scores(b + 1, sb_ref)
        update(b, sa_ref, False)
        scores(b + 2, sa_ref)
        update(b + 1, sb_ref, False)
        return carry

    lax.fori_loop(0, qi // 2, pair, 0)

    @pl.when(qi % 2 == 1)
    def _():
        scores(qi, sb_ref)
        update(qi - 1, sa_ref, False)
        update(qi, sb_ref, True)

    @pl.when(qi % 2 == 0)
    def _():
        update(qi, sa_ref, True)

    acc_a = acc_ref[0]
    acc_b = acc_ref[1]
    o = (acc_a[:, :DA_V_DIM] / acc_a[:, DA_V_DIM:DA_V_DIM + 1]
         - lam * (acc_b[:, :DA_V_DIM] / acc_b[:, DA_V_DIM:DA_V_DIM + 1]))
    o = o * lax.rsqrt(jnp.mean(o * o, axis=-1, keepdims=True) + LN_EPS) * g_ref[...]
    o_ref[0] = (o * (1.0 - lam_init)).astype(o_ref.dtype)


def _diff_attn(h3, slopes, lam, subln_g, *, tq, lam_init):
    bsz, s, _ = h3.shape
    slopes2 = slopes * LOG2E
    qf, kf = _alibi_features(slopes2, tq)
    blk = DA_V_DIM
    q_blk0 = 2 * MIX_A // blk
    k_blk0 = q_blk0 + DA_HEADS
    v_blk0 = k_blk0 + DA_HEADS
    smem = pl.BlockSpec(memory_space=pltpu.SMEM)
    return pl.pallas_call(
        functools.partial(_diff_attn_kernel, tq=tq, lam_init=lam_init),
        grid=(bsz, DA_HEADS, s // tq),
        in_specs=[smem, smem,
                  pl.BlockSpec((1, tq, blk), lambda b, h, i: (b, i, q_blk0 + h)),
                  pl.BlockSpec((1, s, blk), lambda b, h, i: (b, 0, k_blk0 + h)),
                  pl.BlockSpec((1, s, blk), lambda b, h, i: (b, 0, v_blk0 + h)),
                  pl.BlockSpec((1, tq, blk), lambda b, h, i: (h, 0, 0)),
                  pl.BlockSpec((1, tq, blk), lambda b, h, i: (h, 0, 0)),
                  pl.BlockSpec((1, blk), lambda b, h, i: (0, 0))],
        out_specs=pl.BlockSpec((1, tq, blk), lambda b, h, i: (b, i, h)),
        out_shape=jax.ShapeDtypeStruct((bsz, s, DA_HEADS * DA_V_DIM), BF16),
        scratch_shapes=[pltpu.VMEM((2, tq, tq), F32), pltpu.VMEM((2, tq, tq), F32),
                        pltpu.VMEM((2, tq, 2 * DA_V_DIM), F32), pltpu.VMEM((2, tq, 1), F32)],
        compiler_params=_cparams(("parallel", "parallel", "parallel")),
        name="diff_attn",
    )(slopes2, lam, h3, h3, h3, qf, kf, subln_g.reshape(1, blk))


def _retention_kernel(q_ref, k_ref, v_ref, g_ref, gn_ref, o_ref, state_ref):
    tr = q_ref.shape[1]

    @pl.when(pl.program_id(1) == 0)
    def _():
        state_ref[...] = jnp.zeros_like(state_ref)

    idx_r = lax.broadcasted_iota(I32, (RET_CHUNK, RET_CHUNK), 0)
    idx_c = lax.broadcasted_iota(I32, (RET_CHUNK, RET_CHUNK), 1)
    rel = (idx_r - idx_c).astype(F32)
    pos = lax.broadcasted_iota(I32, (RET_CHUNK, 1), 0).astype(F32)
    for h in range(RET_HEADS):
        log_g = math.log1p(-(2.0 ** (-5.0 - h)))
        inner_decay = jnp.where(rel >= 0, jnp.exp(log_g * jnp.maximum(rel, 0.0)), 0.0)
        q_decay = jnp.exp(log_g * (pos + 1.0))
        k_decay = jnp.exp(log_g * (RET_CHUNK - 1.0 - pos))
        chunk_decay = math.exp(log_g * RET_CHUNK)
        kc = slice(h * RET_DK, (h + 1) * RET_DK)
        vc = slice(h * RET_DV, (h + 1) * RET_DV)
        for c0 in range(0, tr, RET_CHUNK):
            rows = slice(c0, c0 + RET_CHUNK)
            q = q_ref[0, rows, kc]
            k = k_ref[0, rows, kc] * (RET_DK ** -0.5)
            v = v_ref[0, rows, vc]
            state = state_ref[h]
            scores = lax.dot_general(q, k, NT_DIMS, preferred_element_type=F32) * inner_decay
            inner = jnp.dot(scores.astype(BF16), v, preferred_element_type=F32)
            cross = jnp.dot(q, state.astype(BF16), preferred_element_type=F32) * q_decay
            k_dec = (k.astype(F32) * k_decay).astype(BF16)
            state_ref[h] = state * chunk_decay + lax.dot_general(
                k_dec, v, TN_DIMS, preferred_element_type=F32)
            o = inner + cross
            mu = jnp.mean(o, axis=-1, keepdims=True)
            oc = o - mu
            var = jnp.mean(oc * oc, axis=-1, keepdims=True)
            on = oc * lax.rsqrt(var + LN_EPS) * gn_ref[h:h + 1, :]
            gate = g_ref[0, rows, vc].astype(F32)
            o_ref[0, rows, vc] = (_silu(gate) * on).astype(o_ref.dtype)


def _retention(h3, gn_g, *, tr):
    bsz, s, _ = h3.shape
    dq = RET_HEADS * RET_DK
    dv = RET_HEADS * RET_DV
    return pl.pallas_call(
        _retention_kernel,
        grid=(bsz, s // tr),
        in_specs=[pl.BlockSpec((1, tr, dq), lambda b, i: (b, i, 0)),
                  pl.BlockSpec((1, tr, dq), lambda b, i: (b, i, 1)),
                  pl.BlockSpec((1, tr, dv), lambda b, i: (b, i, 1)),
                  pl.BlockSpec((1, tr, dv), lambda b, i: (b, i, 2)),
                  pl.BlockSpec(gn_g.shape, lambda b, i: (0, 0))],
        out_specs=pl.BlockSpec((1, tr, dv), lambda b, i: (b, i, 0)),
        out_shape=jax.ShapeDtypeStruct((bsz, s, dv), BF16),
        scratch_shapes=[pltpu.VMEM((RET_HEADS, RET_DK, RET_DV), F32)],
        compiler_params=_cparams(("parallel", "arbitrary")),
        name="retention",
    )(h3, h3, h3, h3, gn_g)


def _swiglu_partial(xb, w1, w3, w2):
    h1 = jnp.dot(xb, w1, preferred_element_type=F32)
    h3 = jnp.dot(xb, w3, preferred_element_type=F32)
    return jnp.dot((_silu(h1) * h3).astype(BF16), w2, preferred_element_type=F32)


def _ffn_kernel(x_ref, w1_ref, w3_ref, w2_ref, g_ref, b_ref, o_ref, xb_ref):
    f = pl.program_id(1)

    @pl.when(f == 0)
    def _():
        xb_ref[...] = x_ref[...].astype(BF16)
        o_ref[...] = jnp.zeros_like(o_ref)

    o_ref[...] += _swiglu_partial(xb_ref[...], w1_ref[...], w3_ref[...], w2_ref[...])

    @pl.when(f == pl.num_programs(1) - 1)
    def _():
        o_ref[...] = _layer_norm(DN_ALPHA * x_ref[...] + o_ref[...], g_ref[...], b_ref[...])


def _ffn(x, w1, w3, w2, g, b, *, tm, tf):
    m, d = x.shape
    return pl.pallas_call(
        _ffn_kernel,
        grid=(m // tm, D_FF // tf),
        in_specs=[pl.BlockSpec((tm, d), lambda i, f: (i, 0)),
                  pl.BlockSpec((d, tf), lambda i, f: (0, f)),
                  pl.BlockSpec((d, tf), lambda i, f: (0, f)),
                  pl.BlockSpec((tf, d), lambda i, f: (f, 0)),
                  pl.BlockSpec((1, d), lambda i, f: (0, 0)),
                  pl.BlockSpec((1, d), lambda i, f: (0, 0))],
        out_specs=pl.BlockSpec((tm, d), lambda i, f: (i, 0)),
        out_shape=jax.ShapeDtypeStruct((m, d), F32),
        scratch_shapes=[pltpu.VMEM((tm, d), BF16)],
        compiler_params=_cparams(("parallel", "arbitrary")),
        name="ffn",
    )(x, w1, w3, w2, g.reshape(1, d), b.reshape(1, d))


def _experts_kernel(te_ref, nv_ref, xs_ref, w1_ref, w3_ref, w2_ref, ys_ref, xb_ref):
    del te_ref
    i = pl.program_id(0)
    f = pl.program_id(1)

    @pl.when(f == 0)
    def _():
        xb_ref[...] = xs_ref[...].astype(BF16)
        ys_ref[...] = jnp.zeros_like(ys_ref)

    @pl.when(i < nv_ref[0])
    def _():
        ys_ref[...] += _swiglu_partial(xb_ref[...], w1_ref[0], w3_ref[0], w2_ref[0])


def _experts(xs, tile_expert, n_valid, w1, w3, w2, *, tm, tf):
    r, d = xs.shape
    nf = D_FF // tf

    def row_tile(i, f, te, nv):
        return jnp.minimum(i, nv[0] - 1)

    def f_tile(i, f, te, nv):
        return jnp.where(i < nv[0], f, nf - 1)

    grid_spec = pltpu.PrefetchScalarGridSpec(
        num_scalar_prefetch=2,
        grid=(r // tm, nf),
        in_specs=[pl.BlockSpec((tm, d), lambda i, f, te, nv: (row_tile(i, f, te, nv), 0)),
                  pl.BlockSpec((1, d, tf), lambda i, f, te, nv: (te[i], 0, f_tile(i, f, te, nv))),
                  pl.BlockSpec((1, d, tf), lambda i, f, te, nv: (te[i], 0, f_tile(i, f, te, nv))),
                  pl.BlockSpec((1, tf, d), lambda i, f, te, nv: (te[i], f_tile(i, f, te, nv), 0))],
        out_specs=pl.BlockSpec((tm, d), lambda i, f, te, nv: (i, 0)),
        scratch_shapes=[pltpu.VMEM((tm, d), BF16)],
    )
    return pl.pallas_call(
        _experts_kernel,
        grid_spec=grid_spec,
        out_shape=jax.ShapeDtypeStruct((r, d), F32),
        compiler_params=_cparams(("arbitrary", "arbitrary")),
        name="experts",
    )(tile_expert, n_valid, xs, w1, w3, w2)


def _split_bf16(x):
    hi = x.astype(BF16)
    return hi, (x - hi.astype(F32)).astype(BF16)


def _router_kernel(x_ref, wt_ref, b_ref, meta_ref, gate_ref, cnt_ref, carry_ref, tri_ref):
    i = pl.program_id(0)
    tm = x_ref.shape[0]

    @pl.when(i == 0)
    def _():
        carry_ref[...] = jnp.zeros_like(carry_ref)
        a = lax.broadcasted_iota(I32, (tm, tm), 0)
        b = lax.broadcasted_iota(I32, (tm, tm), 1)
        tri_ref[...] = (a <= b).astype(BF16)

    xh, xl = _split_bf16(x_ref[...])
    wh, wl = _split_bf16(wt_ref[...])
    logits = (lax.dot_general(wh, xh, NT_DIMS, preferred_element_type=F32)
              + lax.dot_general(wh, xl, NT_DIMS, preferred_element_type=F32)
              + lax.dot_general(wl, xh, NT_DIMS, preferred_element_type=F32)
              + b_ref[...])

    e_iota = lax.broadcasted_iota(I32, logits.shape, 0)
    m1 = jnp.max(logits, axis=0, keepdims=True)
    i1 = jnp.min(jnp.where(logits == m1, e_iota, N_EXPERTS), axis=0, keepdims=True)
    rest = jnp.where(e_iota == i1, -jnp.inf, logits)
    m2 = jnp.max(rest, axis=0, keepdims=True)
    i2 = jnp.min(jnp.where(rest == m2, e_iota, N_EXPERTS), axis=0, keepdims=True)
    e2 = jnp.exp(m2 - m1)
    gate_ref[0:1, :] = 1.0 / (1.0 + e2)
    gate_ref[1:2, :] = e2 / (1.0 + e2)

    pick1 = e_iota == i1
    pick2 = e_iota == i2
    chosen = jnp.where(pick1 | pick2, 1.0, 0.0)
    incl = jnp.dot(chosen.astype(BF16), tri_ref[...], preferred_element_type=F32)
    before = carry_ref[...] + incl - chosen
    meta_ref[0:1, :] = i1
    meta_ref[1:2, :] = i2
    meta_ref[2:3, :] = jnp.sum(jnp.where(pick1, before, 0.0), axis=0, keepdims=True).astype(I32)
    meta_ref[3:4, :] = jnp.sum(jnp.where(pick2, before, 0.0), axis=0, keepdims=True).astype(I32)
    carry_ref[...] += incl[:, tm - 1:tm]
    cnt_ref[...] = jnp.broadcast_to(carry_ref[...], cnt_ref.shape).astype(I32)


def _router(x, w_router, b_router, *, tm):
    m, d = x.shape
    return pl.pallas_call(
        _router_kernel,
        grid=(m // tm,),
        in_specs=[pl.BlockSpec((tm, d), lambda i: (i, 0)),
                  pl.BlockSpec((N_EXPERTS, d), lambda i: (0, 0)),
                  pl.BlockSpec((N_EXPERTS, 1), lambda i: (0, 0))],
        out_specs=[pl.BlockSpec((4, tm), lambda i: (0, i)),
                   pl.BlockSpec((TOP_K, tm), lambda i: (0, i)),
                   pl.BlockSpec((N_EXPERTS, 128), lambda i: (0, 0))],
        out_shape=[jax.ShapeDtypeStruct((4, m), I32),
                   jax.ShapeDtypeStruct((TOP_K, m), F32),
                   jax.ShapeDtypeStruct((N_EXPERTS, 128), I32)],
        scratch_shapes=[pltpu.VMEM((N_EXPERTS, 1), F32), pltpu.VMEM((tm, tm), BF16)],
        compiler_params=_cparams(("arbitrary",)),
        name="router",
    )(x, w_router.T, b_router.reshape(N_EXPERTS, 1))


ROW_DMA_UNROLL = 8


def _for_each_row(tt, fn):
    def trip(t, carry):
        for k in range(ROW_DMA_UNROLL):
            fn(t * ROW_DMA_UNROLL + k, k)
        return carry

    lax.fori_loop(0, tt // ROW_DMA_UNROLL, trip, 0)


def _dispatch_kernel(dest_ref, dest_prev_ref, x_ref, xs_in_ref, xs_ref, sem, *, tt):
    del xs_in_ref
    i = pl.program_id(0)
    slot = i % 2

    def copy(idx_ref, step, r, j, s):
        return pltpu.make_async_copy(x_ref.at[pl.ds(step * tt + r, 1), :],
                                     xs_ref.at[pl.ds(idx_ref[0, 0, j * tt + r], 1), :], sem.at[s])

    def start_all():
        _for_each_row(tt, lambda r, k: [copy(dest_ref, i, r, j, slot).start() for j in range(TOP_K)])

    def wait_all(idx_ref, step, s):
        _for_each_row(tt, lambda r, k: [copy(idx_ref, step, r, j, s).wait() for j in range(TOP_K)])

    start_all()

    @pl.when(i > 0)
    def _():
        wait_all(dest_prev_ref, i - 1, 1 - slot)

    @pl.when(i == pl.num_programs(0) - 1)
    def _():
        wait_all(dest_ref, i, slot)


def _dispatch(x, dest_tiles, rows, *, tt):
    m, d = x.shape
    xs0 = jnp.zeros((rows, d), F32)
    idx_block = (1, 1, TOP_K * tt)
    return pl.pallas_call(
        functools.partial(_dispatch_kernel, tt=tt),
        grid=(m // tt,),
        in_specs=[pl.BlockSpec(idx_block, lambda i: (i, 0, 0), memory_space=pltpu.SMEM),
                  pl.BlockSpec(idx_block, lambda i: (jnp.maximum(i - 1, 0), 0, 0), memory_space=pltpu.SMEM),
                  pl.BlockSpec(memory_space=pl.ANY),
                  pl.BlockSpec(memory_space=pl.ANY)],
        out_specs=pl.BlockSpec(memory_space=pl.ANY),
        out_shape=jax.ShapeDtypeStruct((rows, d), F32),
        scratch_shapes=[pltpu.SemaphoreType.DMA((2,))],
        input_output_aliases={3: 0},
        compiler_params=_cparams(("arbitrary",)),
        name="dispatch",
    )(dest_tiles, dest_tiles, x, xs0)


def _combine_kernel(dest_ref, dest_next_ref, x_ref, gate_ref, g_ref, b_ref, ys_ref, o_ref, buf_ref, sem, *, tt):
    i = pl.program_id(0)
    slot = i % 2

    def copy(idx_ref, r, j, s):
        return pltpu.make_async_copy(ys_ref.at[pl.ds(idx_ref[0, 0, j * tt + r], 1), :],
                                     buf_ref.at[s, j, pl.ds(r, 1), :], sem.at[s])

    def start_all(idx_ref, s):
        _for_each_row(tt, lambda r, k: [copy(idx_ref, r, j, s).start(priority=(k + j) % 2) for j in range(TOP_K)])

    @pl.when(i == 0)
    def _():
        start_all(dest_ref, slot)

    @pl.when(i + 1 < pl.num_programs(0))
    def _():
        start_all(dest_next_ref, 1 - slot)

    _for_each_row(tt, lambda r, k: [copy(dest_ref, r, j, slot).wait() for j in range(TOP_K)])

    y = gate_ref[:, 0:1] * buf_ref[slot, 0] + gate_ref[:, 1:2] * buf_ref[slot, 1]
    o_ref[...] = _layer_norm(DN_ALPHA * x_ref[...] + y, g_ref[...], b_ref[...])


def _combine(x, ys, dest_tiles, gates_t, g, b, *, tt):
    m, d = x.shape
    n = m // tt
    idx_block = (1, 1, TOP_K * tt)
    return pl.pallas_call(
        functools.partial(_combine_kernel, tt=tt),
        grid=(n,),
        in_specs=[pl.BlockSpec(idx_block, lambda i: (i, 0, 0), memory_space=pltpu.SMEM),
                  pl.BlockSpec(idx_block, lambda i: (jnp.minimum(i + 1, n - 1), 0, 0), memory_space=pltpu.SMEM),
                  pl.BlockSpec((tt, d), lambda i: (i, 0)),
                  pl.BlockSpec((tt, TOP_K), lambda i: (i, 0)),
                  pl.BlockSpec((1, d), lambda i: (0, 0)),
                  pl.BlockSpec((1, d), lambda i: (0, 0)),
                  pl.BlockSpec(memory_space=pl.ANY)],
        out_specs=pl.BlockSpec((tt, d), lambda i: (i, 0)),
        out_shape=jax.ShapeDtypeStruct((m, d), F32),
        scratch_shapes=[pltpu.VMEM((2, TOP_K, tt, d), F32), pltpu.SemaphoreType.DMA((2,))],
        compiler_params=_cparams(("arbitrary",)),
        name="combine",
    )(dest_tiles, dest_tiles, x, gates_t, g.reshape(1, d), b.reshape(1, d), ys)


def _moe(x, w_router, b_router, w1, w3, w2, g, b, *, tile):
    m, d = x.shape
    meta, gates, counts = _router(x, w_router, b_router, tm=tile.router)
    counts = counts[:, 0]
    te = tile.expert
    padded = (counts + te - 1) // te * te
    ends = jnp.cumsum(padded)
    starts = ends - padded
    expert_ids = meta[0:TOP_K]
    first_row = sum(jnp.where(expert_ids == e, starts[e], 0) for e in range(N_EXPERTS))
    dest = first_row + meta[TOP_K:2 * TOP_K]
    n_tiles = TOP_K * m // te + N_EXPERTS
    tile_start = jnp.arange(n_tiles, dtype=I32) * te
    tile_expert = jnp.minimum(jnp.sum((tile_start[:, None] >= ends[None, :]).astype(I32), axis=1), N_EXPERTS - 1)
    n_valid = (ends[-1:] // te).astype(I32)

    tt = tile.token
    dest_tiles = dest.reshape(TOP_K, m // tt, tt).transpose(1, 0, 2).reshape(m // tt, 1, TOP_K * tt)
    xs = _dispatch(x, dest_tiles, n_tiles * te, tt=tt)
    ys = _experts(xs, tile_expert, n_valid, w1, w3, w2, tm=te, tf=tile.ff)
    return _combine(x, ys, dest_tiles, gates.T, g, b, tt=tt)


class _Tiles:
    def __init__(self, bsz, seq):
        m = bsz * seq
        self.proj = min(512, m)
        self.sgu = min(512, seq)
        self.attn = min(512, seq)
        self.ret = min(512, seq)
        self.ffn = min(1024, m)
        self.ff = 512
        self.router = min(1024, m)
        self.expert = min(512, m)
        self.token = min(256, m)


def _even_layer(x, p, layer_idx, bsz, seq, tile):
    m, d = x.shape
    q_cols = (2 * MIX_A, 2 * MIX_A + DA_HEADS * 2 * DA_HEAD_DIM)
    h = _matmul(x, p["w_in"], tm=tile.proj, tn=p["w_in"].shape[1],
                scaled_cols=q_cols, scale=DA_HEAD_DIM ** -0.5 * LOG2E)
    a_out = _sgu(h, p["w_s"], p["b_s"], p["v_ln_g"], p["v_ln_b"], tm=tile.sgu)
    lam_init = 0.8 - 0.6 * math.exp(-0.3 * layer_idx)
    lam = (jnp.exp(jnp.sum(p["lam_q1"] * p["lam_k1"])) - jnp.exp(jnp.sum(p["lam_q2"] * p["lam_k2"]))
           + lam_init).reshape(1).astype(F32)
    slopes = 2.0 ** (-8.0 * jnp.arange(1, DA_HEADS + 1, dtype=F32) / DA_HEADS)
    b_out = _diff_attn(h.reshape(bsz, seq, -1), slopes, lam, p["subln_g"], tq=tile.attn, lam_init=lam_init)
    x = _proj_res_ln([a_out, b_out.reshape(m, -1)], p["w_o"], x, p["ln1_g"], p["ln1_b"], tm=tile.proj)
    return _ffn(x, p["ffn_w1"], p["ffn_w3"], p["ffn_w2"], p["ln2_g"], p["ln2_b"], tm=tile.ffn, tf=tile.ff)


def _odd_layer(x, p, bsz, seq, tile):
    m, d = x.shape
    n_in = p["w_in"].shape[1]
    h = _matmul(x, p["w_in"], tm=tile.proj, tn=n_in // 2)
    o = _retention(h.reshape(bsz, seq, n_in), p["gn_g"], tr=tile.ret)
    x = _proj_res_ln([o.reshape(m, -1)], p["w_o"], x, p["ln1_g"], p["ln1_b"], tm=tile.proj)
    return _moe(x, p["router_w"], p["router_b"], p["moe_w1"], p["moe_w3"], p["moe_w2"],
                p["ln2_g"], p["ln2_b"], tile=tile)


def kernel(x, even_w_in, even_w_s, even_b_s, even_v_ln_g, even_v_ln_b, even_lam_q1, even_lam_k1, even_lam_q2, even_lam_k2, even_subln_g, even_w_o, even_ln1_g, even_ln1_b, ffn_w1, ffn_w3, ffn_w2, even_ln2_g, even_ln2_b, odd_w_in, odd_gn_g, odd_w_o, odd_ln1_g, odd_ln1_b, router_w, router_b, moe_w1, moe_w3, moe_w2, odd_ln2_g, odd_ln2_b):
    bsz, seq, d = x.shape
    tile = _Tiles(bsz, seq)
    bf = lambda w: w.astype(BF16)
    xf = x.reshape(bsz * seq, d)
    for layer in range(DEPTH):
        i = layer // 2
        if layer % 2 == 0:
            p = dict(w_in=bf(even_w_in[i]), w_s=even_w_s[i], b_s=even_b_s[i], v_ln_g=even_v_ln_g[i],
                     v_ln_b=even_v_ln_b[i], lam_q1=even_lam_q1[i], lam_k1=even_lam_k1[i],
                     lam_q2=even_lam_q2[i], lam_k2=even_lam_k2[i], subln_g=even_subln_g[i],
                     w_o=bf(even_w_o[i]), ln1_g=even_ln1_g[i], ln1_b=even_ln1_b[i],
                     ffn_w1=bf(ffn_w1[i]), ffn_w3=bf(ffn_w3[i]), ffn_w2=bf(ffn_w2[i]),
                     ln2_g=even_ln2_g[i], ln2_b=even_ln2_b[i])
            xf = _even_layer(xf, p, layer, bsz, seq, tile)
        else:
            p = dict(w_in=bf(odd_w_in[i]), gn_g=odd_gn_g[i], w_o=bf(odd_w_o[i]),
                     ln1_g=odd_ln1_g[i], ln1_b=odd_ln1_b[i], router_w=router_w[i], router_b=router_b[i],
                     moe_w1=bf(moe_w1[i]), moe_w3=bf(moe_w3[i]), moe_w2=bf(moe_w2[i]),
                     ln2_g=odd_ln2_g[i], ln2_b=odd_ln2_b[i])
            xf = _odd_layer(xf, p, bsz, seq, tile)
    return xf.reshape(bsz, seq, d)
```

```python
import functools
import math

import jax
import jax.numpy as jnp
from jax import lax
from jax.experimental import pallas as pl
from jax.experimental.pallas import tpu as pltpu

F32 = jnp.float32
BF16 = jnp.bfloat16
I32 = jnp.int32

D_MODEL = 1024
DEPTH = 4
DN_ALPHA = (2.0 * DEPTH) ** 0.25
LN_EPS = 1e-5

MIX_A = D_MODEL // 2
GM_GROUPS = 4
GM_DIM = MIX_A // GM_GROUPS
GM_CHUNK = 128
DA_HEADS = 4
DA_HEAD_DIM = 64
DA_V_DIM = 128
RET_HEADS = 4
RET_DK = D_MODEL // RET_HEADS
RET_DV = 2 * D_MODEL // RET_HEADS
RET_CHUNK = 128
D_FF = 7 * D_MODEL // 2
N_EXPERTS = 8
TOP_K = 2

V7X_VMEM_LIMIT = 56 * 1024 * 1024
SUBLANES = 8
NEG_BIG = -1e30
LOG2E = 1.4426950408889634

NT_DIMS = (((1,), (1,)), ((), ()))
TN_DIMS = (((0,), (0,)), ((), ()))


def _cparams(sem):
    return pltpu.CompilerParams(dimension_semantics=sem, vmem_limit_bytes=V7X_VMEM_LIMIT)


def _layer_norm(z, g, b):
    mu = jnp.mean(z, axis=-1, keepdims=True)
    zc = z - mu
    var = jnp.mean(zc * zc, axis=-1, keepdims=True)
    return zc * lax.rsqrt(var + LN_EPS) * g + b


def _gelu(x):
    return 0.5 * x * (1.0 + lax.erf(x * (2.0 ** -0.5)))


def _silu(x):
    return x * jax.nn.sigmoid(x)


def _matmul_kernel(x_ref, w_ref, o_ref, *, col_chunk, scaled_cols, scale):
    xb = x_ref[...].astype(BF16)
    for c0 in range(0, o_ref.shape[1], col_chunk):
        y = jnp.dot(xb, w_ref[:, c0:c0 + col_chunk], preferred_element_type=F32)
        if scaled_cols is not None and scaled_cols[0] <= c0 < scaled_cols[1]:
            y = y * scale
        o_ref[:, c0:c0 + col_chunk] = y.astype(o_ref.dtype)


def _matmul(x, w, *, tm, tn, scaled_cols=None, scale=1.0):
    m, k = x.shape
    n = w.shape[1]
    col_chunk = 512
    if scaled_cols is not None:
        assert tn == n and scaled_cols[0] % col_chunk == 0 and scaled_cols[1] % col_chunk == 0
    return pl.pallas_call(
        functools.partial(_matmul_kernel, col_chunk=col_chunk, scaled_cols=scaled_cols, scale=scale),
        grid=(n // tn, m // tm),
        in_specs=[pl.BlockSpec((tm, k), lambda j, i: (i, 0)),
                  pl.BlockSpec((k, tn), lambda j, i: (0, j))],
        out_specs=pl.BlockSpec((tm, tn), lambda j, i: (i, j)),
        out_shape=jax.ShapeDtypeStruct((m, n), BF16),
        compiler_params=_cparams(("parallel", "parallel")),
        name="in_proj",
    )(x, w)


def _proj_res_ln_kernel(*refs, n_lhs, row_chunk):
    a_refs = refs[:n_lhs]
    w_ref, r_ref, g_ref, b_ref, o_ref = refs[n_lhs:]
    for r0 in range(0, r_ref.shape[0], row_chunk):
        rows = slice(r0, r0 + row_chunk)
        y = None
        k0 = 0
        for a_ref in a_refs:
            kk = a_ref.shape[1]
            part = jnp.dot(a_ref[rows, :], w_ref[k0:k0 + kk, :], preferred_element_type=F32)
            y = part if y is None else y + part
            k0 += kk
        o_ref[rows, :] = _layer_norm(DN_ALPHA * r_ref[rows, :] + y, g_ref[...], b_ref[...])


def _proj_res_ln(a_list, w, res, g, b, *, tm):
    m, d = res.shape
    in_specs = [pl.BlockSpec((tm, a.shape[1]), lambda i: (i, 0)) for a in a_list]
    in_specs += [pl.BlockSpec(w.shape, lambda i: (0, 0), pipeline_mode=pl.Buffered(1)),
                 pl.BlockSpec((tm, d), lambda i: (i, 0)),
                 pl.BlockSpec((1, d), lambda i: (0, 0)),
                 pl.BlockSpec((1, d), lambda i: (0, 0))]
    return pl.pallas_call(
        functools.partial(_proj_res_ln_kernel, n_lhs=len(a_list), row_chunk=min(256, tm)),
        grid=(m // tm,),
        in_specs=in_specs,
        out_specs=pl.BlockSpec((tm, d), lambda i: (i, 0)),
        out_shape=jax.ShapeDtypeStruct((m, d), F32),
        compiler_params=_cparams(("parallel",)),
        name="out_proj_ln",
    )(*a_list, w, res, g.reshape(1, d), b.reshape(1, d))


def _sgu_kernel(u_ref, v_ref, ws_ref, bst_ref, lg_ref, lb_ref, o_ref):
    tm = u_ref.shape[0]
    row = lax.broadcasted_iota(I32, (GM_CHUNK, GM_CHUNK), 0)
    col = lax.broadcasted_iota(I32, (GM_CHUNK, GM_CHUNK), 1)
    causal = row >= col
    for g in range(GM_GROUPS):
        w = jnp.where(causal, ws_ref[g], 0.0).astype(BF16)
        bias = bst_ref[:, g:g + 1]
        cols = slice(g * GM_DIM, (g + 1) * GM_DIM)
        for c0 in range(0, tm, GM_CHUNK):
            rows = slice(c0, c0 + GM_CHUNK)
            v = _gelu(v_ref[rows, cols].astype(F32))
            vn = _layer_norm(v, lg_ref[g:g + 1, :], lb_ref[g:g + 1, :])
            mixed = jnp.dot(w, vn.astype(BF16), preferred_element_type=F32) + bias
            u = _gelu(u_ref[rows, cols].astype(F32))
            o_ref[rows, cols] = (u * mixed).astype(o_ref.dtype)


def _sgu(h, w_s, b_s, ln_g, ln_b, *, tm):
    m = h.shape[0]
    return pl.pallas_call(
        _sgu_kernel,
        grid=(m // tm,),
        in_specs=[pl.BlockSpec((tm, MIX_A), lambda i: (i, 0)),
                  pl.BlockSpec((tm, MIX_A), lambda i: (i, 1)),
                  pl.BlockSpec(w_s.shape, lambda i: (0, 0, 0)),
                  pl.BlockSpec((GM_CHUNK, GM_GROUPS), lambda i: (0, 0)),
                  pl.BlockSpec(ln_g.shape, lambda i: (0, 0)),
                  pl.BlockSpec(ln_b.shape, lambda i: (0, 0))],
        out_specs=pl.BlockSpec((tm, MIX_A), lambda i: (i, 0)),
        out_shape=jax.ShapeDtypeStruct((m, MIX_A), BF16),
        compiler_params=_cparams(("parallel",)),
        name="sgu",
    )(h, h, w_s, b_s.T, ln_g, ln_b)


def _alibi_features(slopes2, tq):
    rest = slopes2
    pieces = []
    for _ in range(3):
        piece = rest.astype(BF16).astype(F32)
        pieces.append(piece)
        rest = rest - piece
    pos = jnp.arange(tq, dtype=I32)
    hi = jnp.broadcast_to(((pos >> 4) << 4).astype(F32), (DA_HEADS, tq))
    lo = jnp.broadcast_to((pos & 15).astype(F32), (DA_HEADS, tq))
    qf = jnp.zeros((DA_HEADS, tq, DA_V_DIM), F32)
    kf = jnp.zeros((DA_HEADS, tq, DA_V_DIM), F32)
    for t, piece in enumerate(pieces):
        const = jnp.broadcast_to(piece[:, None], (DA_HEADS, tq))
        a, b = 2 * t, 2 * t + 1
        qf = qf.at[:, :, a].set(hi).at[:, :, b].set(lo)
        kf = kf.at[:, :, a].set(-const).at[:, :, b].set(-const)
        a, b = 6 + 2 * t, 7 + 2 * t
        qf = qf.at[:, :, a].set(const).at[:, :, b].set(const)
        kf = kf.at[:, :, a].set(hi).at[:, :, b].set(lo)
    return qf.astype(BF16), kf.astype(BF16)


def _diff_attn_kernel(sl_ref, lam_ref, q_ref, k_ref, v_ref, qf_ref, kf_ref, g_ref, o_ref,
                      sa_ref, sb_ref, acc_ref, m_ref, *, tq, lam_init):
    h = pl.program_id(1)
    qi = pl.program_id(2)
    slope2 = sl_ref[h]
    lam = lam_ref[0]

    lane = lax.broadcasted_iota(I32, (tq, DA_V_DIM), 1)
    ones_col = jnp.where(lane == 0, 1.0, 0.0).astype(BF16)
    q = q_ref[0]
    zero = jnp.zeros_like(q)
    q_maps = tuple(jnp.concatenate([jnp.where(keep, q, zero), qf_ref[0]], axis=1)
                   for keep in (lane < DA_HEAD_DIM, lane >= DA_HEAD_DIM))
    kf = kf_ref[0]

    row = lax.broadcasted_iota(I32, (tq, tq), 0)
    col = lax.broadcasted_iota(I32, (tq, tq), 1)

    def scores(j, s_ref):
        start = pl.multiple_of(j * tq, tq)
        kj = jnp.concatenate([k_ref[0, pl.ds(start, tq), :], kf], axis=1)
        for c in range(2):
            s_ref[c] = lax.dot_general(q_maps[c], kj, NT_DIMS, preferred_element_type=F32)

    def update(j, s_ref, diagonal):
        start = pl.multiple_of(j * tq, tq)
        vj = jnp.concatenate([v_ref[0, pl.ds(start, tq), :], ones_col], axis=1)
        off = -slope2 * ((qi - j) * tq).astype(F32)
        for c in range(2):
            s = s_ref[c]
            if diagonal:
                s = jnp.where(row >= col, s, NEG_BIG)
            m = m_ref[c]
            m_new = jnp.maximum(m, jnp.max(s, axis=-1, keepdims=True) + off)
            p = jnp.exp2(s - (m_new - off)).astype(BF16)
            acc_ref[c] = jnp.exp2(m - m_new) * acc_ref[c] + jnp.dot(p, vj, preferred_element_type=F32)
            m_ref[c] = m_new

    m_ref[...] = jnp.full(m_ref.shape, NEG_BIG, F32)
    acc_ref[...] = jnp.zeros_like(acc_ref)

    scores(0, sa_ref)

    def pair(t, carry):
        b = 2 * t
        scores(b + 1, sb_ref)
        update(b, sa_ref, False)
        scores(b + 2, sa_ref)
        update(b + 1, sb_ref, False)
        return carry

    lax.fori_loop(0, qi // 2, pair, 0)

    @pl.when(qi % 2 == 1)
    def _():
        scores(qi, sb_ref)
        update(qi - 1, sa_ref, False)
        update(qi, sb_ref, True)

    @pl.when(qi % 2 == 0)
    def _():
        update(qi, sa_ref, True)

    acc_a = acc_ref[0]
    acc_b = acc_ref[1]
    o = (acc_a[:, :DA_V_DIM] / acc_a[:, DA_V_DIM:DA_V_DIM + 1]
         - lam * (acc_b[:, :DA_V_DIM] / acc_b[:, DA_V_DIM:DA_V_DIM + 1]))
    o = o * lax.rsqrt(jnp.mean(o * o, axis=-1, keepdims=True) + LN_EPS) * g_ref[...]
    o_ref[0] = (o * (1.0 - lam_init)).astype(o_ref.dtype)


def _diff_attn(h3, slopes, lam, subln_g, *, tq, lam_init):
    bsz, s, _ = h3.shape
    slopes2 = slopes * LOG2E
    qf, kf = _alibi_features(slopes2, tq)
    blk = DA_V_DIM
    q_blk0 = 2 * MIX_A // blk
    k_blk0 = q_blk0 + DA_HEADS
    v_blk0 = k_blk0 + DA_HEADS
    smem = pl.BlockSpec(memory_space=pltpu.SMEM)
    return pl.pallas_call(
        functools.partial(_diff_attn_kernel, tq=tq, lam_init=lam_init),
        grid=(bsz, DA_HEADS, s // tq),
        in_specs=[smem, smem,
                  pl.BlockSpec((1, tq, blk), lambda b, h, i: (b, i, q_blk0 + h)),
                  pl.BlockSpec((1, s, blk), lambda b, h, i: (b, 0, k_blk0 + h)),
                  pl.BlockSpec((1, s, blk), lambda b, h, i: (b, 0, v_blk0 + h)),
                  pl.BlockSpec((1, tq, blk), lambda b, h, i: (h, 0, 0)),
                  pl.BlockSpec((1, tq, blk), lambda b, h, i: (h, 0, 0)),
                  pl.BlockSpec((1, blk), lambda b, h, i: (0, 0))],
        out_specs=pl.BlockSpec((1, tq, blk), lambda b, h, i: (b, i, h)),
        out_shape=jax.ShapeDtypeStruct((bsz, s, DA_HEADS * DA_V_DIM), BF16),
        scratch_shapes=[pltpu.VMEM((2, tq, tq), F32), pltpu.VMEM((2, tq, tq), F32),
                        pltpu.VMEM((2, tq, 2 * DA_V_DIM), F32), pltpu.VMEM((2, tq, 1), F32)],
        compiler_params=_cparams(("parallel", "parallel", "parallel")),
        name="diff_attn",
    )(slopes2, lam, h3, h3, h3, qf, kf, subln_g.reshape(1, blk))


def _retention_kernel(q_ref, k_ref, v_ref, g_ref, gn_ref, o_ref, state_ref):
    tr = q_ref.shape[1]

    @pl.when(pl.program_id(1) == 0)
    def _():
        state_ref[...] = jnp.zeros_like(state_ref)

    idx_r = lax.broadcasted_iota(I32, (RET_CHUNK, RET_CHUNK), 0)
    idx_c = lax.broadcasted_iota(I32, (RET_CHUNK, RET_CHUNK), 1)
    rel = (idx_r - idx_c).astype(F32)
    pos = lax.broadcasted_iota(I32, (RET_CHUNK, 1), 0).astype(F32)
    for h in range(RET_HEADS):
        log_g = math.log1p(-(2.0 ** (-5.0 - h)))
        inner_decay = jnp.where(rel >= 0, jnp.exp(log_g * jnp.maximum(rel, 0.0)), 0.0)
        q_decay = jnp.exp(log_g * (pos + 1.0))
        k_decay = jnp.exp(log_g * (RET_CHUNK - 1.0 - pos))
        chunk_decay = math.exp(log_g * RET_CHUNK)
        kc = slice(h * RET_DK, (h + 1) * RET_DK)
        vc = slice(h * RET_DV, (h + 1) * RET_DV)
        for c0 in range(0, tr, RET_CHUNK):
            rows = slice(c0, c0 + RET_CHUNK)
            q = q_ref[0, rows, kc]
            k = k_ref[0, rows, kc] * (RET_DK ** -0.5)
            v = v_ref[0, rows, vc]
            state = state_ref[h]
            scores = lax.dot_general(q, k, NT_DIMS, preferred_element_type=F32) * inner_decay
            inner = jnp.dot(scores.astype(BF16), v, preferred_element_type=F32)
            cross = jnp.dot(q, state.astype(BF16), preferred_element_type=F32) * q_decay
            k_dec = (k.astype(F32) * k_decay).astype(BF16)
            state_ref[h] = state * chunk_decay + lax.dot_general(
                k_dec, v, TN_DIMS, preferred_element_type=F32)
            o = inner + cross
            mu = jnp.mean(o, axis=-1, keepdims=True)
            oc = o - mu
            var = jnp.mean(oc * oc, axis=-1, keepdims=True)
            on = oc * lax.rsqrt(var + LN_EPS) * gn_ref[h:h + 1, :]
            gate = g_ref[0, rows, vc].astype(F32)
            o_ref[0, rows, vc] = (_silu(gate) * on).astype(o_ref.dtype)


def _retention(h3, gn_g, *, tr):
    bsz, s, _ = h3.shape
    dq = RET_HEADS * RET_DK
    dv = RET_HEADS * RET_DV
    return pl.pallas_call(
        _retention_kernel,
        grid=(bsz, s // tr),
        in_specs=[pl.BlockSpec((1, tr, dq), lambda b, i: (b, i, 0)),
                  pl.BlockSpec((1, tr, dq), lambda b, i: (b, i, 1)),
                  pl.BlockSpec((1, tr, dv), lambda b, i: (b, i, 1)),
                  pl.BlockSpec((1, tr, dv), lambda b, i: (b, i, 2)),
                  pl.BlockSpec(gn_g.shape, lambda b, i: (0, 0))],
        out_specs=pl.BlockSpec((1, tr, dv), lambda b, i: (b, i, 0)),
        out_shape=jax.ShapeDtypeStruct((bsz, s, dv), BF16),
        scratch_shapes=[pltpu.VMEM((RET_HEADS, RET_DK, RET_DV), F32)],
        compiler_params=_cparams(("parallel", "arbitrary")),
        name="retention",
    )(h3, h3, h3, h3, gn_g)


def _swiglu_partial(xb, w1, w3, w2):
    h1 = jnp.dot(xb, w1, preferred_element_type=F32)
    h3 = jnp.dot(xb, w3, preferred_element_type=F32)
    return jnp.dot((_silu(h1) * h3).astype(BF16), w2, preferred_element_type=F32)


def _ffn_kernel(x_ref, w1_ref, w3_ref, w2_ref, g_ref, b_ref, o_ref, xb_ref):
    f = pl.program_id(1)

    @pl.when(f == 0)
    def _():
        xb_ref[...] = x_ref[...].astype(BF16)
        o_ref[...] = jnp.zeros_like(o_ref)

    o_ref[...] += _swiglu_partial(xb_ref[...], w1_ref[...], w3_ref[...], w2_ref[...])

    @pl.when(f == pl.num_programs(1) - 1)
    def _():
        o_ref[...] = _layer_norm(DN_ALPHA * x_ref[...] + o_ref[...], g_ref[...], b_ref[...])


def _ffn(x, w1, w3, w2, g, b, *, tm, tf):
    m, d = x.shape
    return pl.pallas_call(
        _ffn_kernel,
        grid=(m // tm, D_FF // tf),
        in_specs=[pl.BlockSpec((tm, d), lambda i, f: (i, 0)),
                  pl.BlockSpec((d, tf), lambda i, f: (0, f)),
                  pl.BlockSpec((d, tf), lambda i, f: (0, f)),
                  pl.BlockSpec((tf, d), lambda i, f: (f, 0)),
                  pl.BlockSpec((1, d), lambda i, f: (0, 0)),
                  pl.BlockSpec((1, d), lambda i, f: (0, 0))],
        out_specs=pl.BlockSpec((tm, d), lambda i, f: (i, 0)),
        out_shape=jax.ShapeDtypeStruct((m, d), F32),
        scratch_shapes=[pltpu.VMEM((tm, d), BF16)],
        compiler_params=_cparams(("parallel", "arbitrary")),
        name="ffn",
    )(x, w1, w3, w2, g.reshape(1, d), b.reshape(1, d))


def _experts_kernel(te_ref, nv_ref, xs_ref, w1_ref, w3_ref, w2_ref, ys_ref, xb_ref):
    del te_ref
    i = pl.program_id(0)
    f = pl.program_id(1)

    @pl.when(f == 0)
    def _():
        xb_ref[...] = xs_ref[...].astype(BF16)
        ys_ref[...] = jnp.zeros_like(ys_ref)

    @pl.when(i < nv_ref[0])
    def _():
        ys_ref[...] += _swiglu_partial(xb_ref[...], w1_ref[0], w3_ref[0], w2_ref[0])


def _experts(xs, tile_expert, n_valid, w1, w3, w2, *, tm, tf):
    r, d = xs.shape
    nf = D_FF // tf

    def row_tile(i, f, te, nv):
        return jnp.minimum(i, nv[0] - 1)

    def f_tile(i, f, te, nv):
        return jnp.where(i < nv[0], f, nf - 1)

    grid_spec = pltpu.PrefetchScalarGridSpec(
        num_scalar_prefetch=2,
        grid=(r // tm, nf),
        in_specs=[pl.BlockSpec((tm, d), lambda i, f, te, nv: (row_tile(i, f, te, nv), 0)),
                  pl.BlockSpec((1, d, tf), lambda i, f, te, nv: (te[i], 0, f_tile(i, f, te, nv))),
                  pl.BlockSpec((1, d, tf), lambda i, f, te, nv: (te[i], 0, f_tile(i, f, te, nv))),
                  pl.BlockSpec((1, tf, d), lambda i, f, te, nv: (te[i], f_tile(i, f, te, nv), 0))],
        out_specs=pl.BlockSpec((tm, d), lambda i, f, te, nv: (i, 0)),
        scratch_shapes=[pltpu.VMEM((tm, d), BF16)],
    )
    return pl.pallas_call(
        _experts_kernel,
        grid_spec=grid_spec,
        out_shape=jax.ShapeDtypeStruct((r, d), F32),
        compiler_params=_cparams(("arbitrary", "arbitrary")),
        name="experts",
    )(tile_expert, n_valid, xs, w1, w3, w2)


def _split_bf16(x):
    hi = x.astype(BF16)
    return hi, (x - hi.astype(F32)).astype(BF16)


def _router_kernel(x_ref, wt_ref, b_ref, meta_ref, gate_ref, cnt_ref, carry_ref, tri_ref):
    i = pl.program_id(0)
    tm = x_ref.shape[0]

    @pl.when(i == 0)
    def _():
        carry_ref[...] = jnp.zeros_like(carry_ref)
        a = lax.broadcasted_iota(I32, (tm, tm), 0)
        b = lax.broadcasted_iota(I32, (tm, tm), 1)
        tri_ref[...] = (a <= b).astype(BF16)

    xh, xl = _split_bf16(x_ref[...])
    wh, wl = _split_bf16(wt_ref[...])
    logits = (lax.dot_general(wh, xh, NT_DIMS, preferred_element_type=F32)
              + lax.dot_general(wh, xl, NT_DIMS, preferred_element_type=F32)
              + lax.dot_general(wl, xh, NT_DIMS, preferred_element_type=F32)
              + b_ref[...])

    e_iota = lax.broadcasted_iota(I32, logits.shape, 0)
    m1 = jnp.max(logits, axis=0, keepdims=True)
    i1 = jnp.min(jnp.where(logits == m1, e_iota, N_EXPERTS), axis=0, keepdims=True)
    rest = jnp.where(e_iota == i1, -jnp.inf, logits)
    m2 = jnp.max(rest, axis=0, keepdims=True)
    i2 = jnp.min(jnp.where(rest == m2, e_iota, N_EXPERTS), axis=0, keepdims=True)
    e2 = jnp.exp(m2 - m1)
    gate_ref[0:1, :] = 1.0 / (1.0 + e2)
    gate_ref[1:2, :] = e2 / (1.0 + e2)

    pick1 = e_iota == i1
    pick2 = e_iota == i2
    chosen = jnp.where(pick1 | pick2, 1.0, 0.0)
    incl = jnp.dot(chosen.astype(BF16), tri_ref[...], preferred_element_type=F32)
    before = carry_ref[...] + incl - chosen
    meta_ref[0:1, :] = i1
    meta_ref[1:2, :] = i2
    meta_ref[2:3, :] = jnp.sum(jnp.where(pick1, before, 0.0), axis=0, keepdims=True).astype(I32)
    meta_ref[3:4, :] = jnp.sum(jnp.where(pick2, before, 0.0), axis=0, keepdims=True).astype(I32)
    carry_ref[...] += incl[:, tm - 1:tm]
    cnt_ref[...] = jnp.broadcast_to(carry_ref[...], cnt_ref.shape).astype(I32)


def _router(x, w_router, b_router, *, tm):
    m, d = x.shape
    return pl.pallas_call(
        _router_kernel,
        grid=(m // tm,),
        in_specs=[pl.BlockSpec((tm, d), lambda i: (i, 0)),
                  pl.BlockSpec((N_EXPERTS, d), lambda i: (0, 0)),
                  pl.BlockSpec((N_EXPERTS, 1), lambda i: (0, 0))],
        out_specs=[pl.BlockSpec((4, tm), lambda i: (0, i)),
                   pl.BlockSpec((TOP_K, tm), lambda i: (0, i)),
                   pl.BlockSpec((N_EXPERTS, 128), lambda i: (0, 0))],
        out_shape=[jax.ShapeDtypeStruct((4, m), I32),
                   jax.ShapeDtypeStruct((TOP_K, m), F32),
                   jax.ShapeDtypeStruct((N_EXPERTS, 128), I32)],
        scratch_shapes=[pltpu.VMEM((N_EXPERTS, 1), F32), pltpu.VMEM((tm, tm), BF16)],
        compiler_params=_cparams(("arbitrary",)),
        name="router",
    )(x, w_router.T, b_router.reshape(N_EXPERTS, 1))


ROW_DMA_UNROLL = 8


def _for_each_row(tt, fn):
    def trip(t, carry):
        for k in range(ROW_DMA_UNROLL):
            fn(t * ROW_DMA_UNROLL + k, k)
        return carry

    lax.fori_loop(0, tt // ROW_DMA_UNROLL, trip, 0)


def _dispatch_kernel(dest_ref, dest_prev_ref, pad_ref, x_ref, xs_ref, stage_ref, zero_ref, sem, zero_sem, *, tt):
    i = pl.program_id(0)
    slot = i % 2

    def zero_copy(e):
        start = pl.multiple_of(pad_ref[e], SUBLANES)
        return pltpu.make_async_copy(zero_ref, xs_ref.at[pl.ds(start, zero_ref.shape[0]), :], zero_sem)

    te = zero_ref.shape[0] - SUBLANES
    n_tiles = xs_ref.shape[0] // te

    def tail_copy(t):
        return pltpu.make_async_copy(zero_ref.at[pl.ds(0, te), :],
                                     xs_ref.at[pl.ds(pl.multiple_of(t * te, te), te), :], zero_sem)

    @pl.when(i == 0)
    def _():
        zero_ref[...] = jnp.zeros_like(zero_ref)
        for e in range(N_EXPERTS):
            zero_copy(e).start()
            zero_copy(e).wait()
        for e in range(N_EXPERTS):
            t = pad_ref[N_EXPERTS] + e

            @pl.when(t < n_tiles)
            def _():
                tail_copy(t).start()
                tail_copy(t).wait()

    def copy(idx_ref, r, j, s):
        return pltpu.make_async_copy(stage_ref.at[s, pl.ds(r, 1), :],
                                     xs_ref.at[pl.ds(idx_ref[0, 0, j * tt + r], 1), :], sem.at[s])

    def wait_all(idx_ref, s):
        _for_each_row(tt, lambda r, k: [copy(idx_ref, r, j, s).wait() for j in range(TOP_K)])

    stage_ref[slot] = x_ref[...]
    _for_each_row(tt, lambda r, k: [copy(dest_ref, r, j, slot).start() for j in range(TOP_K)])

    @pl.when(i > 0)
    def _():
        wait_all(dest_prev_ref, 1 - slot)

    @pl.when(i == pl.num_programs(0) - 1)
    def _():
        wait_all(dest_ref, slot)


def _dispatch(x, dest_tiles, pad_start, rows, *, tt, te):
    m, d = x.shape
    idx_block = (1, 1, TOP_K * tt)
    return pl.pallas_call(
        functools.partial(_dispatch_kernel, tt=tt),
        grid=(m // tt,),
        in_specs=[pl.BlockSpec(idx_block, lambda i: (i, 0, 0), memory_space=pltpu.SMEM),
                  pl.BlockSpec(idx_block, lambda i: (jnp.maximum(i - 1, 0), 0, 0), memory_space=pltpu.SMEM),
                  pl.BlockSpec(memory_space=pltpu.SMEM),
                  pl.BlockSpec((tt, d), lambda i: (i, 0))],
        out_specs=pl.BlockSpec(memory_space=pl.ANY),
        out_shape=jax.ShapeDtypeStruct((rows, d), F32),
        scratch_shapes=[pltpu.VMEM((2, tt, d), F32), pltpu.VMEM((te + SUBLANES, d), F32),
                        pltpu.SemaphoreType.DMA((2,)), pltpu.SemaphoreType.DMA(())],
        compiler_params=_cparams(("arbitrary",)),
        name="dispatch",
    )(dest_tiles, dest_tiles, pad_start, x)


def _combine_kernel(dest_ref, dest_next_ref, x_ref, gate_ref, g_ref, b_ref, ys_ref, o_ref, buf_ref, sem, *, tt):
    i = pl.program_id(0)
    slot = i % 2

    def copy(idx_ref, r, j, s):
        return pltpu.make_async_copy(ys_ref.at[pl.ds(idx_ref[0, 0, j * tt + r], 1), :],
                                     buf_ref.at[s, j, pl.ds(r, 1), :], sem.at[s])

    def start_all(idx_ref, s):
        _for_each_row(tt, lambda r, k: [copy(idx_ref, r, j, s).start(priority=(k + j) % 2) for j in range(TOP_K)])

    @pl.when(i == 0)
    def _():
        start_all(dest_ref, slot)

    @pl.when(i + 1 < pl.num_programs(0))
    def _():
        start_all(dest_next_ref, 1 - slot)

    _for_each_row(tt, lambda r, k: [copy(dest_ref, r, j, slot).wait() for j in range(TOP_K)])

    y = gate_ref[:, 0:1] * buf_ref[slot, 0] + gate_ref[:, 1:2] * buf_ref[slot, 1]
    o_ref[...] = _layer_norm(DN_ALPHA * x_ref[...] + y, g_ref[...], b_ref[...])


def _combine(x, ys, dest_tiles, gates_t, g, b, *, tt):
    m, d = x.shape
    n = m // tt
    idx_block = (1, 1, TOP_K * tt)
    return pl.pallas_call(
        functools.partial(_combine_kernel, tt=tt),
        grid=(n,),
        in_specs=[pl.BlockSpec(idx_block, lambda i: (i, 0, 0), memory_space=pltpu.SMEM),
                  pl.BlockSpec(idx_block, lambda i: (jnp.minimum(i + 1, n - 1), 0, 0), memory_space=pltpu.SMEM),
                  pl.BlockSpec((tt, d), lambda i: (i, 0)),
                  pl.BlockSpec((tt, TOP_K), lambda i: (i, 0)),
                  pl.BlockSpec((1, d), lambda i: (0, 0)),
                  pl.BlockSpec((1, d), lambda i: (0, 0)),
                  pl.BlockSpec(memory_space=pl.ANY)],
        out_specs=pl.BlockSpec((tt, d), lambda i: (i, 0)),
        out_shape=jax.ShapeDtypeStruct((m, d), F32),
        scratch_shapes=[pltpu.VMEM((2, TOP_K, tt, d), F32), pltpu.SemaphoreType.DMA((2,))],
        compiler_params=_cparams(("arbitrary",)),
        name="combine",
    )(dest_tiles, dest_tiles, x, gates_t, g.reshape(1, d), b.reshape(1, d), ys)


def _moe(x, w_router, b_router, w1, w3, w2, g, b, *, tile):
    m, d = x.shape
    meta, gates, counts = _router(x, w_router, b_router, tm=tile.router)
    counts = counts[:, 0]
    te = tile.expert
    padded = (counts + te - 1) // te * te
    ends = jnp.cumsum(padded)
    starts = ends - padded
    expert_ids = meta[0:TOP_K]
    first_row = sum(jnp.where(expert_ids == e, starts[e], 0) for e in range(N_EXPERTS))
    dest = first_row + meta[TOP_K:2 * TOP_K]
    n_tiles = TOP_K * m // te + N_EXPERTS
    tile_start = jnp.arange(n_tiles, dtype=I32) * te
    tile_expert = jnp.minimum(jnp.sum((tile_start[:, None] >= ends[None, :]).astype(I32), axis=1), N_EXPERTS - 1)
    n_valid = (ends[-1:] // te).astype(I32)

    rows = n_tiles * te
    pad_start = jnp.minimum((starts + counts) // SUBLANES * SUBLANES, rows - te - SUBLANES).astype(I32)
    pad_start = jnp.concatenate([pad_start, n_valid])

    tt = tile.token
    dest_tiles = dest.reshape(TOP_K, m // tt, tt).transpose(1, 0, 2).reshape(m // tt, 1, TOP_K * tt)
    xs = _dispatch(x, dest_tiles, pad_start, rows, tt=tt, te=te)
    ys = _experts(xs, tile_expert, n_valid, w1, w3, w2, tm=te, tf=tile.ff)
    return _combine(x, ys, dest_tiles, gates.T, g, b, tt=tt)


class _Tiles:
    def __init__(self, bsz, seq):
        m = bsz * seq
        self.proj = min(512, m)
        self.sgu = min(512, seq)
        self.attn = min(512, seq)
        self.ret = min(512, seq)
        self.ffn = min(1024, m)
        self.ff = 512
        self.router = min(1024, m)
        self.expert = min(1024, m)
        self.token = min(256, m)


def _even_layer(x, p, layer_idx, bsz, seq, tile):
    m, d = x.shape
    q_cols = (2 * MIX_A, 2 * MIX_A + DA_HEADS * 2 * DA_HEAD_DIM)
    h = _matmul(x, p["w_in"], tm=tile.proj, tn=p["w_in"].shape[1],
                scaled_cols=q_cols, scale=DA_HEAD_DIM ** -0.5 * LOG2E)
    a_out = _sgu(h, p["w_s"], p["b_s"], p["v_ln_g"], p["v_ln_b"], tm=tile.sgu)
    lam_init = 0.8 - 0.6 * math.exp(-0.3 * layer_idx)
    lam = (jnp.exp(jnp.sum(p["lam_q1"] * p["lam_k1"])) - jnp.exp(jnp.sum(p["lam_q2"] * p["lam_k2"]))
           + lam_init).reshape(1).astype(F32)
    slopes = 2.0 ** (-8.0 * jnp.arange(1, DA_HEADS + 1, dtype=F32) / DA_HEADS)
    b_out = _diff_attn(h.reshape(bsz, seq, -1), slopes, lam, p["subln_g"], tq=tile.attn, lam_init=lam_init)
    x = _proj_res_ln([a_out, b_out.reshape(m, -1)], p["w_o"], x, p["ln1_g"], p["ln1_b"], tm=tile.proj)
    return _ffn(x, p["ffn_w1"], p["ffn_w3"], p["ffn_w2"], p["ln2_g"], p["ln2_b"], tm=tile.ffn, tf=tile.ff)


def _odd_layer(x, p, bsz, seq, tile):
    m, d = x.shape
    n_in = p["w_in"].shape[1]
    h = _matmul(x, p["w_in"], tm=tile.proj, tn=n_in // 2)
    o = _retention(h.reshape(bsz, seq, n_in), p["gn_g"], tr=tile.ret)
    x = _proj_res_ln([o.reshape(m, -1)], p["w_o"], x, p["ln1_g"], p["ln1_b"], tm=tile.proj)
    return _moe(x, p["router_w"], p["router_b"], p["moe_w1"], p["moe_w3"], p["moe_w2"],
                p["ln2_g"], p["ln2_b"], tile=tile)


def kernel(x, even_w_in, even_w_s, even_b_s, even_v_ln_g, even_v_ln_b, even_lam_q1, even_lam_k1, even_lam_q2, even_lam_k2, even_subln_g, even_w_o, even_ln1_g, even_ln1_b, ffn_w1, ffn_w3, ffn_w2, even_ln2_g, even_ln2_b, odd_w_in, odd_gn_g, odd_w_o, odd_ln1_g, odd_ln1_b, router_w, router_b, moe_w1, moe_w3, moe_w2, odd_ln2_g, odd_ln2_b):
    bsz, seq, d = x.shape
    tile = _Tiles(bsz, seq)
    bf = lambda w: w.astype(BF16)
    xf = x.reshape(bsz * seq, d)
    for layer in range(DEPTH):
        i = layer // 2
        if layer % 2 == 0:
            p = dict(w_in=bf(even_w_in[i]), w_s=even_w_s[i], b_s=even_b_s[i], v_ln_g=even_v_ln_g[i],
                     v_ln_b=even_v_ln_b[i], lam_q1=even_lam_q1[i], lam_k1=even_lam_k1[i],
                     lam_q2=even_lam_q2[i], lam_k2=even_lam_k2[i], subln_g=even_subln_g[i],
                     w_o=bf(even_w_o[i]), ln1_g=even_ln1_g[i], ln1_b=even_ln1_b[i],
                     ffn_w1=bf(ffn_w1[i]), ffn_w3=bf(ffn_w3[i]), ffn_w2=bf(ffn_w2[i]),
                     ln2_g=even_ln2_g[i], ln2_b=even_ln2_b[i])
            xf = _even_layer(xf, p, layer, bsz, seq, tile)
        else:
            p = dict(w_in=bf(odd_w_in[i]), gn_g=odd_gn_g[i], w_o=bf(odd_w_o[i]),
                     ln1_g=odd_ln1_g[i], ln1_b=odd_ln1_b[i], router_w=router_w[i], router_b=router_b[i],
                     moe_w1=bf(moe_w1[i]), moe_w3=bf(moe_w3[i]), moe_w2=bf(moe_w2[i]),
                     ln2_g=odd_ln2_g[i], ln2_b=odd_ln2_b[i])
            xf = _odd_layer(xf, p, bsz, seq, tile)
    return xf.reshape(bsz, seq, d)
```

```python
import functools
import math

import jax
import jax.numpy as jnp
from jax import lax
from jax.experimental import pallas as pl
from jax.experimental.pallas import tpu as pltpu

F32 = jnp.float32
BF16 = jnp.bfloat16
I32 = jnp.int32

D_MODEL = 1024
DEPTH = 4
DN_ALPHA = (2.0 * DEPTH) ** 0.25
LN_EPS = 1e-5

MIX_A = D_MODEL // 2
GM_GROUPS = 4
GM_DIM = MIX_A // GM_GROUPS
GM_CHUNK = 128
DA_HEADS = 4
DA_HEAD_DIM = 64
DA_V_DIM = 128
RET_HEADS = 4
RET_DK = D_MODEL // RET_HEADS
RET_DV = 2 * D_MODEL // RET_HEADS
RET_CHUNK = 128
D_FF = 7 * D_MODEL // 2
N_EXPERTS = 8
TOP_K = 2

V7X_VMEM_LIMIT = 56 * 1024 * 1024
SUBLANES = 8
NEG_BIG = -1e30
LOG2E = 1.4426950408889634

NT_DIMS = (((1,), (1,)), ((), ()))
TN_DIMS = (((0,), (0,)), ((), ()))


def _cparams(sem):
    return pltpu.CompilerParams(dimension_semantics=sem, vmem_limit_bytes=V7X_VMEM_LIMIT)


def _layer_norm(z, g, b):
    mu = jnp.mean(z, axis=-1, keepdims=True)
    zc = z - mu
    var = jnp.mean(zc * zc, axis=-1, keepdims=True)
    return zc * lax.rsqrt(var + LN_EPS) * g + b


def _gelu(x):
    return 0.5 * x * (1.0 + lax.erf(x * (2.0 ** -0.5)))


def _silu(x):
    return x * jax.nn.sigmoid(x)


def _matmul_kernel(x_ref, w_ref, o_ref, *, col_chunk, scaled_cols, scale):
    xb = x_ref[...].astype(BF16)
    for c0 in range(0, o_ref.shape[1], col_chunk):
        y = jnp.dot(xb, w_ref[:, c0:c0 + col_chunk], preferred_element_type=F32)
        if scaled_cols is not None and scaled_cols[0] <= c0 < scaled_cols[1]:
            y = y * scale
        o_ref[:, c0:c0 + col_chunk] = y.astype(o_ref.dtype)


def _matmul(x, w, *, tm, tn, scaled_cols=None, scale=1.0):
    m, k = x.shape
    n = w.shape[1]
    col_chunk = 512
    if scaled_cols is not None:
        assert tn == n and scaled_cols[0] % col_chunk == 0 and scaled_cols[1] % col_chunk == 0
    return pl.pallas_call(
        functools.partial(_matmul_kernel, col_chunk=col_chunk, scaled_cols=scaled_cols, scale=scale),
        grid=(n // tn, m // tm),
        in_specs=[pl.BlockSpec((tm, k), lambda j, i: (i, 0)),
                  pl.BlockSpec((k, tn), lambda j, i: (0, j))],
        out_specs=pl.BlockSpec((tm, tn), lambda j, i: (i, j)),
        out_shape=jax.ShapeDtypeStruct((m, n), BF16),
        compiler_params=_cparams(("parallel", "parallel")),
        name="in_proj",
    )(x, w)


def _proj_res_ln_kernel(*refs, n_lhs, row_chunk):
    a_refs = refs[:n_lhs]
    w_ref, r_ref, g_ref, b_ref, o_ref = refs[n_lhs:]
    for r0 in range(0, r_ref.shape[0], row_chunk):
        rows = slice(r0, r0 + row_chunk)
        y = None
        k0 = 0
        for a_ref in a_refs:
            kk = a_ref.shape[1]
            part = jnp.dot(a_ref[rows, :], w_ref[k0:k0 + kk, :], preferred_element_type=F32)
            y = part if y is None else y + part
            k0 += kk
        o_ref[rows, :] = _layer_norm(DN_ALPHA * r_ref[rows, :] + y, g_ref[...], b_ref[...])


def _proj_res_ln(a_list, w, res, g, b, *, tm):
    m, d = res.shape
    in_specs = [pl.BlockSpec((tm, a.shape[1]), lambda i: (i, 0)) for a in a_list]
    in_specs += [pl.BlockSpec(w.shape, lambda i: (0, 0), pipeline_mode=pl.Buffered(1)),
                 pl.BlockSpec((tm, d), lambda i: (i, 0)),
                 pl.BlockSpec((1, d), lambda i: (0, 0)),
                 pl.BlockSpec((1, d), lambda i: (0, 0))]
    return pl.pallas_call(
        functools.partial(_proj_res_ln_kernel, n_lhs=len(a_list), row_chunk=min(256, tm)),
        grid=(m // tm,),
        in_specs=in_specs,
        out_specs=pl.BlockSpec((tm, d), lambda i: (i, 0)),
        out_shape=jax.ShapeDtypeStruct((m, d), F32),
        compiler_params=_cparams(("parallel",)),
        name="out_proj_ln",
    )(*a_list, w, res, g.reshape(1, d), b.reshape(1, d))


def _sgu_kernel(u_ref, v_ref, ws_ref, bst_ref, lg_ref, lb_ref, o_ref):
    tm = u_ref.shape[0]
    row = lax.broadcasted_iota(I32, (GM_CHUNK, GM_CHUNK), 0)
    col = lax.broadcasted_iota(I32, (GM_CHUNK, GM_CHUNK), 1)
    causal = row >= col
    for g in range(GM_GROUPS):
        w = jnp.where(causal, ws_ref[g], 0.0).astype(BF16)
        bias = bst_ref[:, g:g + 1]
        cols = slice(g * GM_DIM, (g + 1) * GM_DIM)
        for c0 in range(0, tm, GM_CHUNK):
            rows = slice(c0, c0 + GM_CHUNK)
            v = _gelu(v_ref[rows, cols].astype(F32))
            vn = _layer_norm(v, lg_ref[g:g + 1, :], lb_ref[g:g + 1, :])
            mixed = jnp.dot(w, vn.astype(BF16), preferred_element_type=F32) + bias
            u = _gelu(u_ref[rows, cols].astype(F32))
            o_ref[rows, cols] = (u * mixed).astype(o_ref.dtype)


def _sgu(h, w_s, b_s, ln_g, ln_b, *, tm):
    m = h.shape[0]
    return pl.pallas_call(
        _sgu_kernel,
        grid=(m // tm,),
        in_specs=[pl.BlockSpec((tm, MIX_A), lambda i: (i, 0)),
                  pl.BlockSpec((tm, MIX_A), lambda i: (i, 1)),
                  pl.BlockSpec(w_s.shape, lambda i: (0, 0, 0)),
                  pl.BlockSpec((GM_CHUNK, GM_GROUPS), lambda i: (0, 0)),
                  pl.BlockSpec(ln_g.shape, lambda i: (0, 0)),
                  pl.BlockSpec(ln_b.shape, lambda i: (0, 0))],
        out_specs=pl.BlockSpec((tm, MIX_A), lambda i: (i, 0)),
        out_shape=jax.ShapeDtypeStruct((m, MIX_A), BF16),
        compiler_params=_cparams(("parallel",)),
        name="sgu",
    )(h, h, w_s, b_s.T, ln_g, ln_b)


def _alibi_features(slopes2, tq):
    rest = slopes2
    pieces = []
    for _ in range(3):
        piece = rest.astype(BF16).astype(F32)
        pieces.append(piece)
        rest = rest - piece
    pos = jnp.arange(tq, dtype=I32)
    hi = jnp.broadcast_to(((pos >> 4) << 4).astype(F32), (DA_HEADS, tq))
    lo = jnp.broadcast_to((pos & 15).astype(F32), (DA_HEADS, tq))
    qf = jnp.zeros((DA_HEADS, tq, DA_V_DIM), F32)
    kf = jnp.zeros((DA_HEADS, tq, DA_V_DIM), F32)
    for t, piece in enumerate(pieces):
        const = jnp.broadcast_to(piece[:, None], (DA_HEADS, tq))
        a, b = 2 * t, 2 * t + 1
        qf = qf.at[:, :, a].set(hi).at[:, :, b].set(lo)
        kf = kf.at[:, :, a].set(-const).at[:, :, b].set(-const)
        a, b = 6 + 2 * t, 7 + 2 * t
        qf = qf.at[:, :, a].set(const).at[:, :, b].set(const)
        kf = kf.at[:, :, a].set(hi).at[:, :, b].set(lo)
    return qf.astype(BF16), kf.astype(BF16)


def _diff_attn_kernel(sl_ref, lam_ref, q_ref, k_ref, v_ref, qf_ref, kf_ref, g_ref, o_ref,
                      sa_ref, sb_ref, acc_ref, m_ref, *, tq, lam_init):
    h = pl.program_id(1)
    qi = pl.program_id(2)
    slope2 = sl_ref[h]
    lam = lam_ref[0]

    lane = lax.broadcasted_iota(I32, (tq, DA_V_DIM), 1)
    ones_col = jnp.where(lane == 0, 1.0, 0.0).astype(BF16)
    q = q_ref[0]
    zero = jnp.zeros_like(q)
    q_maps = tuple(jnp.concatenate([jnp.where(keep, q, zero), qf_ref[0]], axis=1)
                   for keep in (lane < DA_HEAD_DIM, lane >= DA_HEAD_DIM))
    kf = kf_ref[0]

    row = lax.broadcasted_iota(I32, (tq, tq), 0)
    col = lax.broadcasted_iota(I32, (tq, tq), 1)

    def scores(j, s_ref):
        start = pl.multiple_of(j * tq, tq)
        kj = jnp.concatenate([k_ref[0, pl.ds(start, tq), :], kf], axis=1)
        for c in range(2):
            s_ref[c] = lax.dot_general(q_maps[c], kj, NT_DIMS, preferred_element_type=F32)

    def update(j, s_ref, diagonal):
        start = pl.multiple_of(j * tq, tq)
        vj = jnp.concatenate([v_ref[0, pl.ds(start, tq), :], ones_col], axis=1)
        off = -slope2 * ((qi - j) * tq).astype(F32)
        for c in range(2):
            s = s_ref[c]
            if diagonal:
                s = jnp.where(row >= col, s, NEG_BIG)
            m = m_ref[c]
            m_new = jnp.maximum(m, jnp.max(s, axis=-1, keepdims=True) + off)
            p = jnp.exp2(s - (m_new - off)).astype(BF16)
            acc_ref[c] = jnp.exp2(m - m_new) * acc_ref[c] + jnp.dot(p, vj, preferred_element_type=F32)
            m_ref[c] = m_new

    m_ref[...] = jnp.full(m_ref.shape, NEG_BIG, F32)
    acc_ref[...] = jnp.zeros_like(acc_ref)

    scores(0, sa_ref)

    def pair(t, carry):
        b = 2 * t
        scores(b + 1, sb_ref)
        update(b, sa_ref, False)
        scores(b + 2, sa_ref)
        update(b + 1, sb_ref, False)
        return carry

    lax.fori_loop(0, qi // 2, pair, 0)

    @pl.when(qi % 2 == 1)
    def _():
        scores(qi, sb_ref)
        update(qi - 1, sa_ref, False)
        update(qi, sb_ref, True)

    @pl.when(qi % 2 == 0)
    def _():
        update(qi, sa_ref, True)

    acc_a = acc_ref[0]
    acc_b = acc_ref[1]
    o = (acc_a[:, :DA_V_DIM] / acc_a[:, DA_V_DIM:DA_V_DIM + 1]
         - lam * (acc_b[:, :DA_V_DIM] / acc_b[:, DA_V_DIM:DA_V_DIM + 1]))
    o = o * lax.rsqrt(jnp.mean(o * o, axis=-1, keepdims=True) + LN_EPS) * g_ref[...]
    o_ref[0] = (o * (1.0 - lam_init)).astype(o_ref.dtype)


def _diff_attn(h3, slopes, lam, subln_g, *, tq, lam_init):
    bsz, s, _ = h3.shape
    slopes2 = slopes * LOG2E
    qf, kf = _alibi_features(slopes2, tq)
    blk = DA_V_DIM
    q_blk0 = 2 * MIX_A // blk
    k_blk0 = q_blk0 + DA_HEADS
    v_blk0 = k_blk0 + DA_HEADS
    smem = pl.BlockSpec(memory_space=pltpu.SMEM)
    return pl.pallas_call(
        functools.partial(_diff_attn_kernel, tq=tq, lam_init=lam_init),
        grid=(bsz, DA_HEADS, s // tq),
        in_specs=[smem, smem,
                  pl.BlockSpec((1, tq, blk), lambda b, h, i: (b, i, q_blk0 + h)),
                  pl.BlockSpec((1, s, blk), lambda b, h, i: (b, 0, k_blk0 + h)),
                  pl.BlockSpec((1, s, blk), lambda b, h, i: (b, 0, v_blk0 + h)),
                  pl.BlockSpec((1, tq, blk), lambda b, h, i: (h, 0, 0)),
                  pl.BlockSpec((1, tq, blk), lambda b, h, i: (h, 0, 0)),
                  pl.BlockSpec((1, blk), lambda b, h, i: (0, 0))],
        out_specs=pl.BlockSpec((1, tq, blk), lambda b, h, i: (b, i, h)),
        out_shape=jax.ShapeDtypeStruct((bsz, s, DA_HEADS * DA_V_DIM), BF16),
        scratch_shapes=[pltpu.VMEM((2, tq, tq), F32), pltpu.VMEM((2, tq, tq), F32),
                        pltpu.VMEM((2, tq, 2 * DA_V_DIM), F32), pltpu.VMEM((2, tq, 1), F32)],
        compiler_params=_cparams(("parallel", "parallel", "parallel")),
        name="diff_attn",
    )(slopes2, lam, h3, h3, h3, qf, kf, subln_g.reshape(1, blk))


def _retention_kernel(q_ref, k_ref, v_ref, g_ref, gn_ref, o_ref, state_ref):
    tr = q_ref.shape[1]

    @pl.when(pl.program_id(1) == 0)
    def _():
        state_ref[...] = jnp.zeros_like(state_ref)

    idx_r = lax.broadcasted_iota(I32, (RET_CHUNK, RET_CHUNK), 0)
    idx_c = lax.broadcasted_iota(I32, (RET_CHUNK, RET_CHUNK), 1)
    rel = (idx_r - idx_c).astype(F32)
    pos = lax.broadcasted_iota(I32, (RET_CHUNK, 1), 0).astype(F32)
    for h in range(RET_HEADS):
        log_g = math.log1p(-(2.0 ** (-5.0 - h)))
        inner_decay = jnp.where(rel >= 0, jnp.exp(log_g * jnp.maximum(rel, 0.0)), 0.0)
        q_decay = jnp.exp(log_g * (pos + 1.0))
        k_decay = jnp.exp(log_g * (RET_CHUNK - 1.0 - pos))
        chunk_decay = math.exp(log_g * RET_CHUNK)
        kc = slice(h * RET_DK, (h + 1) * RET_DK)
        vc = slice(h * RET_DV, (h + 1) * RET_DV)
        for c0 in range(0, tr, RET_CHUNK):
            rows = slice(c0, c0 + RET_CHUNK)
            q = q_ref[0, rows, kc]
            k = k_ref[0, rows, kc] * (RET_DK ** -0.5)
            v = v_ref[0, rows, vc]
            state = state_ref[h]
            scores = lax.dot_general(q, k, NT_DIMS, preferred_element_type=F32) * inner_decay
            inner = jnp.dot(scores.astype(BF16), v, preferred_element_type=F32)
            cross = jnp.dot(q, state.astype(BF16), preferred_element_type=F32) * q_decay
            k_dec = (k.astype(F32) * k_decay).astype(BF16)
            state_ref[h] = state * chunk_decay + lax.dot_general(
                k_dec, v, TN_DIMS, preferred_element_type=F32)
            o = inner + cross
            mu = jnp.mean(o, axis=-1, keepdims=True)
            oc = o - mu
            var = jnp.mean(oc * oc, axis=-1, keepdims=True)
            on = oc * lax.rsqrt(var + LN_EPS) * gn_ref[h:h + 1, :]
            gate = g_ref[0, rows, vc].astype(F32)
            o_ref[0, rows, vc] = (_silu(gate) * on).astype(o_ref.dtype)


def _retention(h3, gn_g, *, tr):
    bsz, s, _ = h3.shape
    dq = RET_HEADS * RET_DK
    dv = RET_HEADS * RET_DV
    return pl.pallas_call(
        _retention_kernel,
        grid=(bsz, s // tr),
        in_specs=[pl.BlockSpec((1, tr, dq), lambda b, i: (b, i, 0)),
                  pl.BlockSpec((1, tr, dq), lambda b, i: (b, i, 1)),
                  pl.BlockSpec((1, tr, dv), lambda b, i: (b, i, 1)),
                  pl.BlockSpec((1, tr, dv), lambda b, i: (b, i, 2)),
                  pl.BlockSpec(gn_g.shape, lambda b, i: (0, 0))],
        out_specs=pl.BlockSpec((1, tr, dv), lambda b, i: (b, i, 0)),
        out_shape=jax.ShapeDtypeStruct((bsz, s, dv), BF16),
        scratch_shapes=[pltpu.VMEM((RET_HEADS, RET_DK, RET_DV), F32)],
        compiler_params=_cparams(("parallel", "arbitrary")),
        name="retention",
    )(h3, h3, h3, h3, gn_g)


def _swiglu(xb, w1, w3, w2, tf):
    acc = None
    for f0 in range(0, w2.shape[0], tf):
        h1 = jnp.dot(xb, w1[:, f0:f0 + tf], preferred_element_type=F32)
        h3 = jnp.dot(xb, w3[:, f0:f0 + tf], preferred_element_type=F32)
        part = jnp.dot((_silu(h1) * h3).astype(BF16), w2[f0:f0 + tf, :], preferred_element_type=F32)
        acc = part if acc is None else acc + part
    return acc


def _ffn_kernel(x_ref, w1_ref, w3_ref, w2_ref, g_ref, b_ref, o_ref, *, tf, row_chunk):
    for r0 in range(0, x_ref.shape[0], row_chunk):
        rows = slice(r0, r0 + row_chunk)
        x = x_ref[rows, :]
        y = _swiglu(x.astype(BF16), w1_ref, w3_ref, w2_ref, tf)
        o_ref[rows, :] = _layer_norm(DN_ALPHA * x + y, g_ref[...], b_ref[...])


def _ffn(x, w1, w3, w2, g, b, *, tm, tf):
    m, d = x.shape
    resident = dict(pipeline_mode=pl.Buffered(1))
    return pl.pallas_call(
        functools.partial(_ffn_kernel, tf=tf, row_chunk=min(512, tm)),
        grid=(m // tm,),
        in_specs=[pl.BlockSpec((tm, d), lambda i: (i, 0)),
                  pl.BlockSpec(w1.shape, lambda i: (0, 0), **resident),
                  pl.BlockSpec(w3.shape, lambda i: (0, 0), **resident),
                  pl.BlockSpec(w2.shape, lambda i: (0, 0), **resident),
                  pl.BlockSpec((1, d), lambda i: (0, 0)),
                  pl.BlockSpec((1, d), lambda i: (0, 0))],
        out_specs=pl.BlockSpec((tm, d), lambda i: (i, 0)),
        out_shape=jax.ShapeDtypeStruct((m, d), F32),
        compiler_params=_cparams(("parallel",)),
        name="ffn",
    )(x, w1, w3, w2, g.reshape(1, d), b.reshape(1, d))


def _experts_kernel(te_ref, nv_ref, xs_ref, w1_ref, w3_ref, w2_ref, ys_ref, *, tf):
    del te_ref
    i = pl.program_id(0)

    @pl.when(i < nv_ref[0])
    def _():
        ys_ref[...] = _swiglu(xs_ref[...].astype(BF16), w1_ref.at[0], w3_ref.at[0], w2_ref.at[0], tf)

    @pl.when(i >= nv_ref[0])
    def _():
        ys_ref[...] = jnp.zeros_like(ys_ref)


def _experts(xs, tile_expert, n_valid, w1, w3, w2, *, tm, tf):
    r, d = xs.shape
    resident = dict(pipeline_mode=pl.Buffered(1))
    grid_spec = pltpu.PrefetchScalarGridSpec(
        num_scalar_prefetch=2,
        grid=(r // tm,),
        in_specs=[pl.BlockSpec((tm, d), lambda i, te, nv: (jnp.minimum(i, nv[0] - 1), 0)),
                  pl.BlockSpec((1,) + w1.shape[1:], lambda i, te, nv: (te[i], 0, 0), **resident),
                  pl.BlockSpec((1,) + w3.shape[1:], lambda i, te, nv: (te[i], 0, 0), **resident),
                  pl.BlockSpec((1,) + w2.shape[1:], lambda i, te, nv: (te[i], 0, 0), **resident)],
        out_specs=pl.BlockSpec((tm, d), lambda i, te, nv: (i, 0)),
    )
    return pl.pallas_call(
        functools.partial(_experts_kernel, tf=tf),
        grid_spec=grid_spec,
        out_shape=jax.ShapeDtypeStruct((r, d), F32),
        compiler_params=_cparams(("arbitrary",)),
        name="experts",
    )(tile_expert, n_valid, xs, w1, w3, w2)


def _split_bf16(x):
    hi = x.astype(BF16)
    return hi, (x - hi.astype(F32)).astype(BF16)


def _router_kernel(x_ref, wt_ref, b_ref, meta_ref, gate_ref, cnt_ref, carry_ref, tri_ref):
    i = pl.program_id(0)
    tm = x_ref.shape[0]

    @pl.when(i == 0)
    def _():
        carry_ref[...] = jnp.zeros_like(carry_ref)
        a = lax.broadcasted_iota(I32, (tm, tm), 0)
        b = lax.broadcasted_iota(I32, (tm, tm), 1)
        tri_ref[...] = (a <= b).astype(BF16)

    xh, xl = _split_bf16(x_ref[...])
    wh, wl = _split_bf16(wt_ref[...])
    logits = (lax.dot_general(wh, xh, NT_DIMS, preferred_element_type=F32)
              + lax.dot_general(wh, xl, NT_DIMS, preferred_element_type=F32)
              + lax.dot_general(wl, xh, NT_DIMS, preferred_element_type=F32)
              + b_ref[...])

    e_iota = lax.broadcasted_iota(I32, logits.shape, 0)
    m1 = jnp.max(logits, axis=0, keepdims=True)
    i1 = jnp.min(jnp.where(logits == m1, e_iota, N_EXPERTS), axis=0, keepdims=True)
    rest = jnp.where(e_iota == i1, -jnp.inf, logits)
    m2 = jnp.max(rest, axis=0, keepdims=True)
    i2 = jnp.min(jnp.where(rest == m2, e_iota, N_EXPERTS), axis=0, keepdims=True)
    e2 = jnp.exp(m2 - m1)
    gate_ref[0:1, :] = 1.0 / (1.0 + e2)
    gate_ref[1:2, :] = e2 / (1.0 + e2)

    pick1 = e_iota == i1
    pick2 = e_iota == i2
    chosen = jnp.where(pick1 | pick2, 1.0, 0.0)
    incl = jnp.dot(chosen.astype(BF16), tri_ref[...], preferred_element_type=F32)
    before = carry_ref[...] + incl - chosen
    meta_ref[0:1, :] = i1
    meta_ref[1:2, :] = i2
    meta_ref[2:3, :] = jnp.sum(jnp.where(pick1, before, 0.0), axis=0, keepdims=True).astype(I32)
    meta_ref[3:4, :] = jnp.sum(jnp.where(pick2, before, 0.0), axis=0, keepdims=True).astype(I32)
    carry_ref[...] += incl[:, tm - 1:tm]
    cnt_ref[...] = jnp.broadcast_to(carry_ref[...], cnt_ref.shape).astype(I32)


def _router(x, w_router, b_router, *, tm):
    m, d = x.shape
    return pl.pallas_call(
        _router_kernel,
        grid=(m // tm,),
        in_specs=[pl.BlockSpec((tm, d), lambda i: (i, 0)),
                  pl.BlockSpec((N_EXPERTS, d), lambda i: (0, 0)),
                  pl.BlockSpec((N_EXPERTS, 1), lambda i: (0, 0))],
        out_specs=[pl.BlockSpec((4, tm), lambda i: (0, i)),
                   pl.BlockSpec((TOP_K, tm), lambda i: (0, i)),
                   pl.BlockSpec((N_EXPERTS, 128), lambda i: (0, 0))],
        out_shape=[jax.ShapeDtypeStruct((4, m), I32),
                   jax.ShapeDtypeStruct((TOP_K, m), F32),
                   jax.ShapeDtypeStruct((N_EXPERTS, 128), I32)],
        scratch_shapes=[pltpu.VMEM((N_EXPERTS, 1), F32), pltpu.VMEM((tm, tm), BF16)],
        compiler_params=_cparams(("arbitrary",)),
        name="router",
    )(x, w_router.T, b_router.reshape(N_EXPERTS, 1))


ROW_DMA_UNROLL = 8


def _for_each_row(tt, fn):
    def trip(t, carry):
        for k in range(ROW_DMA_UNROLL):
            fn(t * ROW_DMA_UNROLL + k, k)
        return carry

    lax.fori_loop(0, tt // ROW_DMA_UNROLL, trip, 0)


def _dispatch_kernel(dest_ref, dest_prev_ref, pad_ref, x_ref, xs_ref, stage_ref, zero_ref, sem, zero_sem, *, tt):
    i = pl.program_id(0)
    slot = i % 2

    def zero_copy(e):
        start = pl.multiple_of(pad_ref[e], SUBLANES)
        return pltpu.make_async_copy(zero_ref, xs_ref.at[pl.ds(start, zero_ref.shape[0]), :], zero_sem)

    te = zero_ref.shape[0] - SUBLANES
    n_tiles = xs_ref.shape[0] // te

    def tail_copy(t):
        return pltpu.make_async_copy(zero_ref.at[pl.ds(0, te), :],
                                     xs_ref.at[pl.ds(pl.multiple_of(t * te, te), te), :], zero_sem)

    @pl.when(i == 0)
    def _():
        zero_ref[...] = jnp.zeros_like(zero_ref)
        for e in range(N_EXPERTS):
            zero_copy(e).start()
            zero_copy(e).wait()
        for e in range(N_EXPERTS):
            t = pad_ref[N_EXPERTS] + e

            @pl.when(t < n_tiles)
            def _():
                tail_copy(t).start()
                tail_copy(t).wait()

    def copy(idx_ref, r, j, s):
        return pltpu.make_async_copy(stage_ref.at[s, pl.ds(r, 1), :],
                                     xs_ref.at[pl.ds(idx_ref[0, 0, j * tt + r], 1), :], sem.at[s])

    def wait_all(idx_ref, s):
        _for_each_row(tt, lambda r, k: [copy(idx_ref, r, j, s).wait() for j in range(TOP_K)])

    stage_ref[slot] = x_ref[...]
    _for_each_row(tt, lambda r, k: [copy(dest_ref, r, j, slot).start() for j in range(TOP_K)])

    @pl.when(i > 0)
    def _():
        wait_all(dest_prev_ref, 1 - slot)

    @pl.when(i == pl.num_programs(0) - 1)
    def _():
        wait_all(dest_ref, slot)


def _dispatch(x, dest_tiles, pad_start, rows, *, tt, te):
    m, d = x.shape
    idx_block = (1, 1, TOP_K * tt)
    return pl.pallas_call(
        functools.partial(_dispatch_kernel, tt=tt),
        grid=(m // tt,),
        in_specs=[pl.BlockSpec(idx_block, lambda i: (i, 0, 0), memory_space=pltpu.SMEM),
                  pl.BlockSpec(idx_block, lambda i: (jnp.maximum(i - 1, 0), 0, 0), memory_space=pltpu.SMEM),
                  pl.BlockSpec(memory_space=pltpu.SMEM),
                  pl.BlockSpec((tt, d), lambda i: (i, 0))],
        out_specs=pl.BlockSpec(memory_space=pl.ANY),
        out_shape=jax.ShapeDtypeStruct((rows, d), F32),
        scratch_shapes=[pltpu.VMEM((2, tt, d), F32), pltpu.VMEM((te + SUBLANES, d), F32),
                        pltpu.SemaphoreType.DMA((2,)), pltpu.SemaphoreType.DMA(())],
        compiler_params=_cparams(("arbitrary",)),
        name="dispatch",
    )(dest_tiles, dest_tiles, pad_start, x)


def _combine_kernel(dest_ref, dest_next_ref, x_ref, gate_ref, g_ref, b_ref, ys_ref, o_ref, buf_ref, sem, *, tt):
    i = pl.program_id(0)
    slot = i % 2

    def copy(idx_ref, r, j, s):
        return pltpu.make_async_copy(ys_ref.at[pl.ds(idx_ref[0, 0, j * tt + r], 1), :],
                                     buf_ref.at[s, j, pl.ds(r, 1), :], sem.at[s])

    def start_all(idx_ref, s):
        _for_each_row(tt, lambda r, k: [copy(idx_ref, r, j, s).start(priority=(k + j) % 2) for j in range(TOP_K)])

    @pl.when(i == 0)
    def _():
        start_all(dest_ref, slot)

    @pl.when(i + 1 < pl.num_programs(0))
    def _():
        start_all(dest_next_ref, 1 - slot)

    _for_each_row(tt, lambda r, k: [copy(dest_ref, r, j, slot).wait() for j in range(TOP_K)])

    y = gate_ref[:, 0:1] * buf_ref[slot, 0] + gate_ref[:, 1:2] * buf_ref[slot, 1]
    o_ref[...] = _layer_norm(DN_ALPHA * x_ref[...] + y, g_ref[...], b_ref[...])


def _combine(x, ys, dest_tiles, gates_t, g, b, *, tt):
    m, d = x.shape
    n = m // tt
    idx_block = (1, 1, TOP_K * tt)
    return pl.pallas_call(
        functools.partial(_combine_kernel, tt=tt),
        grid=(n,),
        in_specs=[pl.BlockSpec(idx_block, lambda i: (i, 0, 0), memory_space=pltpu.SMEM),
                  pl.BlockSpec(idx_block, lambda i: (jnp.minimum(i + 1, n - 1), 0, 0), memory_space=pltpu.SMEM),
                  pl.BlockSpec((tt, d), lambda i: (i, 0)),
                  pl.BlockSpec((tt, TOP_K), lambda i: (i, 0)),
                  pl.BlockSpec((1, d), lambda i: (0, 0)),
                  pl.BlockSpec((1, d), lambda i: (0, 0)),
                  pl.BlockSpec(memory_space=pl.ANY)],
        out_specs=pl.BlockSpec((tt, d), lambda i: (i, 0)),
        out_shape=jax.ShapeDtypeStruct((m, d), F32),
        scratch_shapes=[pltpu.VMEM((2, TOP_K, tt, d), F32), pltpu.SemaphoreType.DMA((2,))],
        compiler_params=_cparams(("arbitrary",)),
        name="combine",
    )(dest_tiles, dest_tiles, x, gates_t, g.reshape(1, d), b.reshape(1, d), ys)


def _moe(x, w_router, b_router, w1, w3, w2, g, b, *, tile):
    m, d = x.shape
    meta, gates, counts = _router(x, w_router, b_router, tm=tile.router)
    counts = counts[:, 0]
    te = tile.expert
    padded = (counts + te - 1) // te * te
    ends = jnp.cumsum(padded)
    starts = ends - padded
    expert_ids = meta[0:TOP_K]
    first_row = sum(jnp.where(expert_ids == e, starts[e], 0) for e in range(N_EXPERTS))
    dest = first_row + meta[TOP_K:2 * TOP_K]
    n_tiles = TOP_K * m // te + N_EXPERTS
    tile_start = jnp.arange(n_tiles, dtype=I32) * te
    tile_expert = jnp.minimum(jnp.sum((tile_start[:, None] >= ends[None, :]).astype(I32), axis=1), N_EXPERTS - 1)
    n_valid = (ends[-1:] // te).astype(I32)

    rows = n_tiles * te
    pad_start = jnp.minimum((starts + counts) // SUBLANES * SUBLANES, rows - te - SUBLANES).astype(I32)
    pad_start = jnp.concatenate([pad_start, n_valid])

    tt = tile.token
    dest_tiles = dest.reshape(TOP_K, m // tt, tt).transpose(1, 0, 2).reshape(m // tt, 1, TOP_K * tt)
    xs = _dispatch(x, dest_tiles, pad_start, rows, tt=tt, te=te)
    ys = _experts(xs, tile_expert, n_valid, w1, w3, w2, tm=te, tf=tile.ff)
    return _combine(x, ys, dest_tiles, gates.T, g, b, tt=tt)


class _Tiles:
    def __init__(self, bsz, seq):
        m = bsz * seq
        self.proj = min(512, m)
        self.sgu = min(512, seq)
        self.attn = min(512, seq)
        self.ret = min(512, seq)
        self.ffn = min(1024, m)
        self.ff = 512
        self.router = min(1024, m)
        self.expert = min(1024, m)
        self.token = min(256, m)


def _even_layer(x, p, layer_idx, bsz, seq, tile):
    m, d = x.shape
    q_cols = (2 * MIX_A, 2 * MIX_A + DA_HEADS * 2 * DA_HEAD_DIM)
    h = _matmul(x, p["w_in"], tm=tile.proj, tn=p["w_in"].shape[1],
                scaled_cols=q_cols, scale=DA_HEAD_DIM ** -0.5 * LOG2E)
    a_out = _sgu(h, p["w_s"], p["b_s"], p["v_ln_g"], p["v_ln_b"], tm=tile.sgu)
    lam_init = 0.8 - 0.6 * math.exp(-0.3 * layer_idx)
    lam = (jnp.exp(jnp.sum(p["lam_q1"] * p["lam_k1"])) - jnp.exp(jnp.sum(p["lam_q2"] * p["lam_k2"]))
           + lam_init).reshape(1).astype(F32)
    slopes = 2.0 ** (-8.0 * jnp.arange(1, DA_HEADS + 1, dtype=F32) / DA_HEADS)
    b_out = _diff_attn(h.reshape(bsz, seq, -1), slopes, lam, p["subln_g"], tq=tile.attn, lam_init=lam_init)
    x = _proj_res_ln([a_out, b_out.reshape(m, -1)], p["w_o"], x, p["ln1_g"], p["ln1_b"], tm=tile.proj)
    return _ffn(x, p["ffn_w1"], p["ffn_w3"], p["ffn_w2"], p["ln2_g"], p["ln2_b"], tm=tile.ffn, tf=tile.ff)


def _odd_layer(x, p, bsz, seq, tile):
    m, d = x.shape
    n_in = p["w_in"].shape[1]
    h = _matmul(x, p["w_in"], tm=tile.proj, tn=n_in // 2)
    o = _retention(h.reshape(bsz, seq, n_in), p["gn_g"], tr=tile.ret)
    x = _proj_res_ln([o.reshape(m, -1)], p["w_o"], x, p["ln1_g"], p["ln1_b"], tm=tile.proj)
    return _moe(x, p["router_w"], p["router_b"], p["moe_w1"], p["moe_w3"], p["moe_w2"],
                p["ln2_g"], p["ln2_b"], tile=tile)


def kernel(x, even_w_in, even_w_s, even_b_s, even_v_ln_g, even_v_ln_b, even_lam_q1, even_lam_k1, even_lam_q2, even_lam_k2, even_subln_g, even_w_o, even_ln1_g, even_ln1_b, ffn_w1, ffn_w3, ffn_w2, even_ln2_g, even_ln2_b, odd_w_in, odd_gn_g, odd_w_o, odd_ln1_g, odd_ln1_b, router_w, router_b, moe_w1, moe_w3, moe_w2, odd_ln2_g, odd_ln2_b):
    bsz, seq, d = x.shape
    tile = _Tiles(bsz, seq)
    bf = lambda w: w.astype(BF16)
    xf = x.reshape(bsz * seq, d)
    for layer in range(DEPTH):
        i = layer // 2
        if layer % 2 == 0:
            p = dict(w_in=bf(even_w_in[i]), w_s=even_w_s[i], b_s=even_b_s[i], v_ln_g=even_v_ln_g[i],
                     v_ln_b=even_v_ln_b[i], lam_q1=even_lam_q1[i], lam_k1=even_lam_k1[i],
                     lam_q2=even_lam_q2[i], lam_k2=even_lam_k2[i], subln_g=even_subln_g[i],
                     w_o=bf(even_w_o[i]), ln1_g=even_ln1_g[i], ln1_b=even_ln1_b[i],
                     ffn_w1=bf(ffn_w1[i]), ffn_w3=bf(ffn_w3[i]), ffn_w2=bf(ffn_w2[i]),
                     ln2_g=even_ln2_g[i], ln2_b=even_ln2_b[i])
            xf = _even_layer(xf, p, layer, bsz, seq, tile)
        else:
            p = dict(w_in=bf(odd_w_in[i]), gn_g=odd_gn_g[i], w_o=bf(odd_w_o[i]),
                     ln1_g=odd_ln1_g[i], ln1_b=odd_ln1_b[i], router_w=router_w[i], router_b=router_b[i],
                     moe_w1=bf(moe_w1[i]), moe_w3=bf(moe_w3[i]), moe_w2=bf(moe_w2[i]),
                     ln2_g=odd_ln2_g[i], ln2_b=odd_ln2_b[i])
            xf = _odd_layer(xf, p, bsz, seq, tile)
    return xf.reshape(bsz, seq, d)
```

```python
import functools
import math

import jax
import jax.numpy as jnp
from jax import lax
from jax.experimental import pallas as pl
from jax.experimental.pallas import tpu as pltpu

F32 = jnp.float32
BF16 = jnp.bfloat16
I32 = jnp.int32

D_MODEL = 1024
DEPTH = 4
DN_ALPHA = (2.0 * DEPTH) ** 0.25
LN_EPS = 1e-5

MIX_A = D_MODEL // 2
GM_GROUPS = 4
GM_DIM = MIX_A // GM_GROUPS
GM_CHUNK = 128
DA_HEADS = 4
DA_HEAD_DIM = 64
DA_V_DIM = 128
RET_HEADS = 4
RET_DK = D_MODEL // RET_HEADS
RET_DV = 2 * D_MODEL // RET_HEADS
RET_CHUNK = 128
D_FF = 7 * D_MODEL // 2
N_EXPERTS = 8
TOP_K = 2

V7X_VMEM_LIMIT = 56 * 1024 * 1024
SUBLANES = 8
NEG_BIG = -1e30
LOG2E = 1.4426950408889634

NT_DIMS = (((1,), (1,)), ((), ()))
TN_DIMS = (((0,), (0,)), ((), ()))


def _cparams(sem):
    return pltpu.CompilerParams(dimension_semantics=sem, vmem_limit_bytes=V7X_VMEM_LIMIT)


def _layer_norm(z, g, b):
    mu = jnp.mean(z, axis=-1, keepdims=True)
    zc = z - mu
    var = jnp.mean(zc * zc, axis=-1, keepdims=True)
    return zc * lax.rsqrt(var + LN_EPS) * g + b


def _gelu(x):
    return 0.5 * x * (1.0 + lax.erf(x * (2.0 ** -0.5)))


def _silu(x):
    return x * jax.nn.sigmoid(x)


def _matmul_kernel(x_ref, w_ref, o_ref, *, col_chunk, scaled_cols, scale):
    xb = x_ref[...].astype(BF16)
    for c0 in range(0, o_ref.shape[1], col_chunk):
        y = jnp.dot(xb, w_ref[:, c0:c0 + col_chunk], preferred_element_type=F32)
        if scaled_cols is not None and scaled_cols[0] <= c0 < scaled_cols[1]:
            y = y * scale
        o_ref[:, c0:c0 + col_chunk] = y.astype(o_ref.dtype)


def _matmul(x, w, *, tm, tn, scaled_cols=None, scale=1.0):
    m, k = x.shape
    n = w.shape[1]
    col_chunk = 512
    if scaled_cols is not None:
        assert tn == n and scaled_cols[0] % col_chunk == 0 and scaled_cols[1] % col_chunk == 0
    return pl.pallas_call(
        functools.partial(_matmul_kernel, col_chunk=col_chunk, scaled_cols=scaled_cols, scale=scale),
        grid=(n // tn, m // tm),
        in_specs=[pl.BlockSpec((tm, k), lambda j, i: (i, 0)),
                  pl.BlockSpec((k, tn), lambda j, i: (0, j))],
        out_specs=pl.BlockSpec((tm, tn), lambda j, i: (i, j)),
        out_shape=jax.ShapeDtypeStruct((m, n), BF16),
        compiler_params=_cparams(("parallel", "parallel")),
        name="in_proj",
    )(x, w)


def _proj_res_ln_kernel(*refs, n_lhs, row_chunk):
    a_refs = refs[:n_lhs]
    w_ref, r_ref, g_ref, b_ref, o_ref = refs[n_lhs:]
    for r0 in range(0, r_ref.shape[0], row_chunk):
        rows = slice(r0, r0 + row_chunk)
        y = None
        k0 = 0
        for a_ref in a_refs:
            kk = a_ref.shape[1]
            part = jnp.dot(a_ref[rows, :], w_ref[k0:k0 + kk, :], preferred_element_type=F32)
            y = part if y is None else y + part
            k0 += kk
        o_ref[rows, :] = _layer_norm(DN_ALPHA * r_ref[rows, :] + y, g_ref[...], b_ref[...])


def _proj_res_ln(a_list, w, res, g, b, *, tm):
    m, d = res.shape
    in_specs = [pl.BlockSpec((tm, a.shape[1]), lambda i: (i, 0)) for a in a_list]
    in_specs += [pl.BlockSpec(w.shape, lambda i: (0, 0), pipeline_mode=pl.Buffered(1)),
                 pl.BlockSpec((tm, d), lambda i: (i, 0)),
                 pl.BlockSpec((1, d), lambda i: (0, 0)),
                 pl.BlockSpec((1, d), lambda i: (0, 0))]
    return pl.pallas_call(
        functools.partial(_proj_res_ln_kernel, n_lhs=len(a_list), row_chunk=min(256, tm)),
        grid=(m // tm,),
        in_specs=in_specs,
        out_specs=pl.BlockSpec((tm, d), lambda i: (i, 0)),
        out_shape=jax.ShapeDtypeStruct((m, d), F32),
        compiler_params=_cparams(("parallel",)),
        name="out_proj_ln",
    )(*a_list, w, res, g.reshape(1, d), b.reshape(1, d))


def _sgu_kernel(u_ref, v_ref, ws_ref, bst_ref, lg_ref, lb_ref, o_ref):
    tm = u_ref.shape[0]
    row = lax.broadcasted_iota(I32, (GM_CHUNK, GM_CHUNK), 0)
    col = lax.broadcasted_iota(I32, (GM_CHUNK, GM_CHUNK), 1)
    causal = row >= col
    for g in range(GM_GROUPS):
        w = jnp.where(causal, ws_ref[g], 0.0).astype(BF16)
        bias = bst_ref[:, g:g + 1]
        cols = slice(g * GM_DIM, (g + 1) * GM_DIM)
        for c0 in range(0, tm, GM_CHUNK):
            rows = slice(c0, c0 + GM_CHUNK)
            v = _gelu(v_ref[rows, cols].astype(F32))
            vn = _layer_norm(v, lg_ref[g:g + 1, :], lb_ref[g:g + 1, :])
            mixed = jnp.dot(w, vn.astype(BF16), preferred_element_type=F32) + bias
            u = _gelu(u_ref[rows, cols].astype(F32))
            o_ref[rows, cols] = (u * mixed).astype(o_ref.dtype)


def _sgu(h, w_s, b_s, ln_g, ln_b, *, tm):
    m = h.shape[0]
    return pl.pallas_call(
        _sgu_kernel,
        grid=(m // tm,),
        in_specs=[pl.BlockSpec((tm, MIX_A), lambda i: (i, 0)),
                  pl.BlockSpec((tm, MIX_A), lambda i: (i, 1)),
                  pl.BlockSpec(w_s.shape, lambda i: (0, 0, 0)),
                  pl.BlockSpec((GM_CHUNK, GM_GROUPS), lambda i: (0, 0)),
                  pl.BlockSpec(ln_g.shape, lambda i: (0, 0)),
                  pl.BlockSpec(ln_b.shape, lambda i: (0, 0))],
        out_specs=pl.BlockSpec((tm, MIX_A), lambda i: (i, 0)),
        out_shape=jax.ShapeDtypeStruct((m, MIX_A), BF16),
        compiler_params=_cparams(("parallel",)),
        name="sgu",
    )(h, h, w_s, b_s.T, ln_g, ln_b)


def _alibi_features(slopes2, tq):
    rest = slopes2
    pieces = []
    for _ in range(3):
        piece = rest.astype(BF16).astype(F32)
        pieces.append(piece)
        rest = rest - piece
    pos = jnp.arange(tq, dtype=I32)
    hi = jnp.broadcast_to(((pos >> 4) << 4).astype(F32), (DA_HEADS, tq))
    lo = jnp.broadcast_to((pos & 15).astype(F32), (DA_HEADS, tq))
    qf = jnp.zeros((DA_HEADS, tq, DA_V_DIM), F32)
    kf = jnp.zeros((DA_HEADS, tq, DA_V_DIM), F32)
    for t, piece in enumerate(pieces):
        const = jnp.broadcast_to(piece[:, None], (DA_HEADS, tq))
        a, b = 2 * t, 2 * t + 1
        qf = qf.at[:, :, a].set(hi).at[:, :, b].set(lo)
        kf = kf.at[:, :, a].set(-const).at[:, :, b].set(-const)
        a, b = 6 + 2 * t, 7 + 2 * t
        qf = qf.at[:, :, a].set(const).at[:, :, b].set(const)
        kf = kf.at[:, :, a].set(hi).at[:, :, b].set(lo)
    return qf.astype(BF16), kf.astype(BF16)


def _diff_attn_kernel(sl_ref, lam_ref, q_ref, k_ref, v_ref, qf_ref, kf_ref, g_ref, o_ref,
                      sa_ref, sb_ref, acc_ref, m_ref, *, tq, lam_init):
    h = pl.program_id(1)
    qi = pl.program_id(2)
    slope2 = sl_ref[h]
    lam = lam_ref[0]

    lane = lax.broadcasted_iota(I32, (tq, DA_V_DIM), 1)
    ones_col = jnp.where(lane == 0, 1.0, 0.0).astype(BF16)
    q = q_ref[0]
    zero = jnp.zeros_like(q)
    q_maps = tuple(jnp.concatenate([jnp.where(keep, q, zero), qf_ref[0]], axis=1)
                   for keep in (lane < DA_HEAD_DIM, lane >= DA_HEAD_DIM))
    kf = kf_ref[0]

    row = lax.broadcasted_iota(I32, (tq, tq), 0)
    col = lax.broadcasted_iota(I32, (tq, tq), 1)

    def scores(j, s_ref):
        start = pl.multiple_of(j * tq, tq)
        kj = jnp.concatenate([k_ref[0, pl.ds(start, tq), :], kf], axis=1)
        for c in range(2):
            s_ref[c] = lax.dot_general(q_maps[c], kj, NT_DIMS, preferred_element_type=F32)

    def update(j, s_ref, diagonal):
        start = pl.multiple_of(j * tq, tq)
        vj = jnp.concatenate([v_ref[0, pl.ds(start, tq), :], ones_col], axis=1)
        off = -slope2 * ((qi - j) * tq).astype(F32)
        for c in range(2):
            s = s_ref[c]
            if diagonal:
                s = jnp.where(row >= col, s, NEG_BIG)
            m = m_ref[c]
            m_new = jnp.maximum(m, jnp.max(s, axis=-1, keepdims=True) + off)
            p = jnp.exp2(s - (m_new - off)).astype(BF16)
            acc_ref[c] = jnp.exp2(m - m_new) * acc_ref[c] + jnp.dot(p, vj, preferred_element_type=F32)
            m_ref[c] = m_new

    m_ref[...] = jnp.full(m_ref.shape, NEG_BIG, F32)
    acc_ref[...] = jnp.zeros_like(acc_ref)

    scores(0, sa_ref)

    def pair(t, carry):
        b = 2 * t
        scores(b + 1, sb_ref)
        update(b, sa_ref, False)
        scores(b + 2, sa_ref)
        update(b + 1, sb_ref, False)
        return carry

    lax.fori_loop(0, qi // 2, pair, 0)

    @pl.when(qi % 2 == 1)
    def _():
        scores(qi, sb_ref)
        update(qi - 1, sa_ref, False)
        update(qi, sb_ref, True)

    @pl.when(qi % 2 == 0)
    def _():
        update(qi, sa_ref, True)

    acc_a = acc_ref[0]
    acc_b = acc_ref[1]
    o = (acc_a[:, :DA_V_DIM] / acc_a[:, DA_V_DIM:DA_V_DIM + 1]
         - lam * (acc_b[:, :DA_V_DIM] / acc_b[:, DA_V_DIM:DA_V_DIM + 1]))
    o = o * lax.rsqrt(jnp.mean(o * o, axis=-1, keepdims=True) + LN_EPS) * g_ref[...]
    o_ref[0] = (o * (1.0 - lam_init)).astype(o_ref.dtype)


def _diff_attn(h3, slopes, lam, subln_g, *, tq, lam_init):
    bsz, s, _ = h3.shape
    slopes2 = slopes * LOG2E
    qf, kf = _alibi_features(slopes2, tq)
    blk = DA_V_DIM
    q_blk0 = 2 * MIX_A // blk
    k_blk0 = q_blk0 + DA_HEADS
    v_blk0 = k_blk0 + DA_HEADS
    smem = pl.BlockSpec(memory_space=pltpu.SMEM)
    return pl.pallas_call(
        functools.partial(_diff_attn_kernel, tq=tq, lam_init=lam_init),
        grid=(bsz, DA_HEADS, s // tq),
        in_specs=[smem, smem,
                  pl.BlockSpec((1, tq, blk), lambda b, h, i: (b, i, q_blk0 + h)),
                  pl.BlockSpec((1, s, blk), lambda b, h, i: (b, 0, k_blk0 + h)),
                  pl.BlockSpec((1, s, blk), lambda b, h, i: (b, 0, v_blk0 + h)),
                  pl.BlockSpec((1, tq, blk), lambda b, h, i: (h, 0, 0)),
                  pl.BlockSpec((1, tq, blk), lambda b, h, i: (h, 0, 0)),
                  pl.BlockSpec((1, blk), lambda b, h, i: (0, 0))],
        out_specs=pl.BlockSpec((1, tq, blk), lambda b, h, i: (b, i, h)),
        out_shape=jax.ShapeDtypeStruct((bsz, s, DA_HEADS * DA_V_DIM), BF16),
        scratch_shapes=[pltpu.VMEM((2, tq, tq), F32), pltpu.VMEM((2, tq, tq), F32),
                        pltpu.VMEM((2, tq, 2 * DA_V_DIM), F32), pltpu.VMEM((2, tq, 1), F32)],
        compiler_params=_cparams(("parallel", "parallel", "parallel")),
        name="diff_attn",
    )(slopes2, lam, h3, h3, h3, qf, kf, subln_g.reshape(1, blk))


def _retention_kernel(q_ref, k_ref, v_ref, g_ref, gn_ref, o_ref, state_ref):
    tr = q_ref.shape[1]

    @pl.when(pl.program_id(1) == 0)
    def _():
        state_ref[...] = jnp.zeros_like(state_ref)

    idx_r = lax.broadcasted_iota(I32, (RET_CHUNK, RET_CHUNK), 0)
    idx_c = lax.broadcasted_iota(I32, (RET_CHUNK, RET_CHUNK), 1)
    rel = (idx_r - idx_c).astype(F32)
    pos = lax.broadcasted_iota(I32, (RET_CHUNK, 1), 0).astype(F32)
    for h in range(RET_HEADS):
        log_g = math.log1p(-(2.0 ** (-5.0 - h)))
        inner_decay = jnp.where(rel >= 0, jnp.exp(log_g * jnp.maximum(rel, 0.0)), 0.0)
        q_decay = jnp.exp(log_g * (pos + 1.0))
        k_decay = jnp.exp(log_g * (RET_CHUNK - 1.0 - pos))
        chunk_decay = math.exp(log_g * RET_CHUNK)
        kc = slice(h * RET_DK, (h + 1) * RET_DK)
        vc = slice(h * RET_DV, (h + 1) * RET_DV)
        for c0 in range(0, tr, RET_CHUNK):
            rows = slice(c0, c0 + RET_CHUNK)
            q = q_ref[0, rows, kc]
            k = k_ref[0, rows, kc] * (RET_DK ** -0.5)
            v = v_ref[0, rows, vc]
            state = state_ref[h]
            scores = lax.dot_general(q, k, NT_DIMS, preferred_element_type=F32) * inner_decay
            inner = jnp.dot(scores.astype(BF16), v, preferred_element_type=F32)
            cross = jnp.dot(q, state.astype(BF16), preferred_element_type=F32) * q_decay
            k_dec = (k.astype(F32) * k_decay).astype(BF16)
            state_ref[h] = state * chunk_decay + lax.dot_general(
                k_dec, v, TN_DIMS, preferred_element_type=F32)
            o = inner + cross
            mu = jnp.mean(o, axis=-1, keepdims=True)
            oc = o - mu
            var = jnp.mean(oc * oc, axis=-1, keepdims=True)
            on = oc * lax.rsqrt(var + LN_EPS) * gn_ref[h:h + 1, :]
            gate = g_ref[0, rows, vc].astype(F32)
            o_ref[0, rows, vc] = (_silu(gate) * on).astype(o_ref.dtype)


def _retention(h3, gn_g, *, tr):
    bsz, s, _ = h3.shape
    dq = RET_HEADS * RET_DK
    dv = RET_HEADS * RET_DV
    return pl.pallas_call(
        _retention_kernel,
        grid=(bsz, s // tr),
        in_specs=[pl.BlockSpec((1, tr, dq), lambda b, i: (b, i, 0)),
                  pl.BlockSpec((1, tr, dq), lambda b, i: (b, i, 1)),
                  pl.BlockSpec((1, tr, dv), lambda b, i: (b, i, 1)),
                  pl.BlockSpec((1, tr, dv), lambda b, i: (b, i, 2)),
                  pl.BlockSpec(gn_g.shape, lambda b, i: (0, 0))],
        out_specs=pl.BlockSpec((1, tr, dv), lambda b, i: (b, i, 0)),
        out_shape=jax.ShapeDtypeStruct((bsz, s, dv), BF16),
        scratch_shapes=[pltpu.VMEM((RET_HEADS, RET_DK, RET_DV), F32)],
        compiler_params=_cparams(("parallel", "arbitrary")),
        name="retention",
    )(h3, h3, h3, h3, gn_g)


def _swiglu(xb, w1, w3, w2, tf):
    acc = None
    for f0 in range(0, w2.shape[0], tf):
        h1 = jnp.dot(xb, w1[:, f0:f0 + tf], preferred_element_type=F32)
        h3 = jnp.dot(xb, w3[:, f0:f0 + tf], preferred_element_type=F32)
        part = jnp.dot((_silu(h1) * h3).astype(BF16), w2[f0:f0 + tf, :], preferred_element_type=F32)
        acc = part if acc is None else acc + part
    return acc


def _ffn_kernel(x_ref, w1_ref, w3_ref, w2_ref, g_ref, b_ref, o_ref, *, tf, row_chunk):
    for r0 in range(0, x_ref.shape[0], row_chunk):
        rows = slice(r0, r0 + row_chunk)
        x = x_ref[rows, :]
        y = _swiglu(x.astype(BF16), w1_ref, w3_ref, w2_ref, tf)
        o_ref[rows, :] = _layer_norm(DN_ALPHA * x + y, g_ref[...], b_ref[...])


def _ffn(x, w1, w3, w2, g, b, *, tm, tf):
    m, d = x.shape
    resident = dict(pipeline_mode=pl.Buffered(1))
    return pl.pallas_call(
        functools.partial(_ffn_kernel, tf=tf, row_chunk=min(512, tm)),
        grid=(m // tm,),
        in_specs=[pl.BlockSpec((tm, d), lambda i: (i, 0)),
                  pl.BlockSpec(w1.shape, lambda i: (0, 0), **resident),
                  pl.BlockSpec(w3.shape, lambda i: (0, 0), **resident),
                  pl.BlockSpec(w2.shape, lambda i: (0, 0), **resident),
                  pl.BlockSpec((1, d), lambda i: (0, 0)),
                  pl.BlockSpec((1, d), lambda i: (0, 0))],
        out_specs=pl.BlockSpec((tm, d), lambda i: (i, 0)),
        out_shape=jax.ShapeDtypeStruct((m, d), F32),
        compiler_params=_cparams(("parallel",)),
        name="ffn",
    )(x, w1, w3, w2, g.reshape(1, d), b.reshape(1, d))


def _experts_kernel(te_ref, nv_ref, xs_ref, w1_ref, w3_ref, w2_ref, ys_ref, *, tf):
    del te_ref
    i = pl.program_id(0)

    @pl.when(i < nv_ref[0])
    def _():
        ys_ref[...] = _swiglu(xs_ref[...].astype(BF16), w1_ref.at[0, 0], w3_ref.at[0, 0], w2_ref.at[0, 0], tf)

    @pl.when(i >= nv_ref[0])
    def _():
        ys_ref[...] = jnp.zeros_like(ys_ref)


def _experts(xs, tile_expert, n_valid, w1, w3, w2, layer, *, tm, tf):
    r, d = xs.shape
    resident = dict(pipeline_mode=pl.Buffered(1))

    def expert_block(w):
        return pl.BlockSpec((1, 1) + w.shape[2:], lambda i, te, nv: (layer, te[i], 0, 0), **resident)

    grid_spec = pltpu.PrefetchScalarGridSpec(
        num_scalar_prefetch=2,
        grid=(r // tm,),
        in_specs=[pl.BlockSpec((tm, d), lambda i, te, nv: (jnp.minimum(i, nv[0] - 1), 0)),
                  expert_block(w1), expert_block(w3), expert_block(w2)],
        out_specs=pl.BlockSpec((tm, d), lambda i, te, nv: (i, 0)),
    )
    return pl.pallas_call(
        functools.partial(_experts_kernel, tf=tf),
        grid_spec=grid_spec,
        out_shape=jax.ShapeDtypeStruct((r, d), F32),
        compiler_params=_cparams(("arbitrary",)),
        name="experts",
    )(tile_expert, n_valid, xs, w1, w3, w2)


def _split_bf16(x):
    hi = x.astype(BF16)
    return hi, (x - hi.astype(F32)).astype(BF16)


def _router_kernel(x_ref, wt_ref, b_ref, meta_ref, gate_ref, cnt_ref, carry_ref, tri_ref):
    i = pl.program_id(0)
    tm = x_ref.shape[0]

    @pl.when(i == 0)
    def _():
        carry_ref[...] = jnp.zeros_like(carry_ref)
        a = lax.broadcasted_iota(I32, (tm, tm), 0)
        b = lax.broadcasted_iota(I32, (tm, tm), 1)
        tri_ref[...] = (a <= b).astype(BF16)

    xh, xl = _split_bf16(x_ref[...])
    wh, wl = _split_bf16(wt_ref[...])
    logits = (lax.dot_general(wh, xh, NT_DIMS, preferred_element_type=F32)
              + lax.dot_general(wh, xl, NT_DIMS, preferred_element_type=F32)
              + lax.dot_general(wl, xh, NT_DIMS, preferred_element_type=F32)
              + b_ref[...])

    e_iota = lax.broadcasted_iota(I32, logits.shape, 0)
    m1 = jnp.max(logits, axis=0, keepdims=True)
    i1 = jnp.min(jnp.where(logits == m1, e_iota, N_EXPERTS), axis=0, keepdims=True)
    rest = jnp.where(e_iota == i1, -jnp.inf, logits)
    m2 = jnp.max(rest, axis=0, keepdims=True)
    i2 = jnp.min(jnp.where(rest == m2, e_iota, N_EXPERTS), axis=0, keepdims=True)
    e2 = jnp.exp(m2 - m1)
    gate_ref[0:1, :] = 1.0 / (1.0 + e2)
    gate_ref[1:2, :] = e2 / (1.0 + e2)

    pick1 = e_iota == i1
    pick2 = e_iota == i2
    chosen = jnp.where(pick1 | pick2, 1.0, 0.0)
    incl = jnp.dot(chosen.astype(BF16), tri_ref[...], preferred_element_type=F32)
    before = carry_ref[...] + incl - chosen
    meta_ref[0:1, :] = i1
    meta_ref[1:2, :] = i2
    meta_ref[2:3, :] = jnp.sum(jnp.where(pick1, before, 0.0), axis=0, keepdims=True).astype(I32)
    meta_ref[3:4, :] = jnp.sum(jnp.where(pick2, before, 0.0), axis=0, keepdims=True).astype(I32)
    carry_ref[...] += incl[:, tm - 1:tm]
    cnt_ref[...] = jnp.broadcast_to(carry_ref[...], cnt_ref.shape).astype(I32)


def _router(x, w_router, b_router, *, tm):
    m, d = x.shape
    return pl.pallas_call(
        _router_kernel,
        grid=(m // tm,),
        in_specs=[pl.BlockSpec((tm, d), lambda i: (i, 0)),
                  pl.BlockSpec((N_EXPERTS, d), lambda i: (0, 0)),
                  pl.BlockSpec((N_EXPERTS, 1), lambda i: (0, 0))],
        out_specs=[pl.BlockSpec((4, tm), lambda i: (0, i)),
                   pl.BlockSpec((TOP_K, tm), lambda i: (0, i)),
                   pl.BlockSpec((N_EXPERTS, 128), lambda i: (0, 0))],
        out_shape=[jax.ShapeDtypeStruct((4, m), I32),
                   jax.ShapeDtypeStruct((TOP_K, m), F32),
                   jax.ShapeDtypeStruct((N_EXPERTS, 128), I32)],
        scratch_shapes=[pltpu.VMEM((N_EXPERTS, 1), F32), pltpu.VMEM((tm, tm), BF16)],
        compiler_params=_cparams(("arbitrary",)),
        name="router",
    )(x, w_router.T, b_router.reshape(N_EXPERTS, 1))


ROW_DMA_UNROLL = 8


def _for_each_row(tt, fn):
    def trip(t, carry):
        for k in range(ROW_DMA_UNROLL):
            fn(t * ROW_DMA_UNROLL + k, k)
        return carry

    lax.fori_loop(0, tt // ROW_DMA_UNROLL, trip, 0)


def _dispatch_kernel(dest_ref, dest_prev_ref, pad_ref, x_ref, xs_ref, stage_ref, zero_ref, sem, zero_sem, *, tt):
    i = pl.program_id(0)
    slot = i % 2

    def zero_copy(e):
        start = pl.multiple_of(pad_ref[e], SUBLANES)
        return pltpu.make_async_copy(zero_ref, xs_ref.at[pl.ds(start, zero_ref.shape[0]), :], zero_sem)

    te = zero_ref.shape[0] - SUBLANES
    n_tiles = xs_ref.shape[0] // te

    def tail_copy(t):
        return pltpu.make_async_copy(zero_ref.at[pl.ds(0, te), :],
                                     xs_ref.at[pl.ds(pl.multiple_of(t * te, te), te), :], zero_sem)

    @pl.when(i == 0)
    def _():
        zero_ref[...] = jnp.zeros_like(zero_ref)
        for e in range(N_EXPERTS):
            zero_copy(e).start()
            zero_copy(e).wait()
        for e in range(N_EXPERTS):
            t = pad_ref[N_EXPERTS] + e

            @pl.when(t < n_tiles)
            def _():
                tail_copy(t).start()
                tail_copy(t).wait()

    def copy(idx_ref, r, j, s):
        return pltpu.make_async_copy(stage_ref.at[s, pl.ds(r, 1), :],
                                     xs_ref.at[pl.ds(idx_ref[0, 0, j * tt + r], 1), :], sem.at[s])

    def wait_all(idx_ref, s):
        _for_each_row(tt, lambda r, k: [copy(idx_ref, r, j, s).wait() for j in range(TOP_K)])

    stage_ref[slot] = x_ref[...]
    _for_each_row(tt, lambda r, k: [copy(dest_ref, r, j, slot).start() for j in range(TOP_K)])

    @pl.when(i > 0)
    def _():
        wait_all(dest_prev_ref, 1 - slot)

    @pl.when(i == pl.num_programs(0) - 1)
    def _():
        wait_all(dest_ref, slot)


def _dispatch(x, dest_tiles, pad_start, rows, *, tt, te):
    m, d = x.shape
    idx_block = (1, 1, TOP_K * tt)
    return pl.pallas_call(
        functools.partial(_dispatch_kernel, tt=tt),
        grid=(m // tt,),
        in_specs=[pl.BlockSpec(idx_block, lambda i: (i, 0, 0), memory_space=pltpu.SMEM),
                  pl.BlockSpec(idx_block, lambda i: (jnp.maximum(i - 1, 0), 0, 0), memory_space=pltpu.SMEM),
                  pl.BlockSpec(memory_space=pltpu.SMEM),
                  pl.BlockSpec((tt, d), lambda i: (i, 0))],
        out_specs=pl.BlockSpec(memory_space=pl.ANY),
        out_shape=jax.ShapeDtypeStruct((rows, d), F32),
        scratch_shapes=[pltpu.VMEM((2, tt, d), F32), pltpu.VMEM((te + SUBLANES, d), F32),
                        pltpu.SemaphoreType.DMA((2,)), pltpu.SemaphoreType.DMA(())],
        compiler_params=_cparams(("arbitrary",)),
        name="dispatch",
    )(dest_tiles, dest_tiles, pad_start, x)


def _combine_kernel(dest_ref, dest_next_ref, x_ref, gate_ref, g_ref, b_ref, ys_ref, o_ref, buf_ref, sem, *, tt):
    i = pl.program_id(0)
    slot = i % 2

    def copy(idx_ref, r, j, s):
        return pltpu.make_async_copy(ys_ref.at[pl.ds(idx_ref[0, 0, j * tt + r], 1), :],
                                     buf_ref.at[s, j, pl.ds(r, 1), :], sem.at[s])

    def start_all(idx_ref, s):
        _for_each_row(tt, lambda r, k: [copy(idx_ref, r, j, s).start(priority=(k + j) % 2) for j in range(TOP_K)])

    @pl.when(i == 0)
    def _():
        start_all(dest_ref, slot)

    @pl.when(i + 1 < pl.num_programs(0))
    def _():
        start_all(dest_next_ref, 1 - slot)

    _for_each_row(tt, lambda r, k: [copy(dest_ref, r, j, slot).wait() for j in range(TOP_K)])

    y = gate_ref[:, 0:1] * buf_ref[slot, 0] + gate_ref[:, 1:2] * buf_ref[slot, 1]
    o_ref[...] = _layer_norm(DN_ALPHA * x_ref[...] + y, g_ref[...], b_ref[...])


def _combine(x, ys, dest_tiles, gates_t, g, b, *, tt):
    m, d = x.shape
    n = m // tt
    idx_block = (1, 1, TOP_K * tt)
    return pl.pallas_call(
        functools.partial(_combine_kernel, tt=tt),
        grid=(n,),
        in_specs=[pl.BlockSpec(idx_block, lambda i: (i, 0, 0), memory_space=pltpu.SMEM),
                  pl.BlockSpec(idx_block, lambda i: (jnp.minimum(i + 1, n - 1), 0, 0), memory_space=pltpu.SMEM),
                  pl.BlockSpec((tt, d), lambda i: (i, 0)),
                  pl.BlockSpec((tt, TOP_K), lambda i: (i, 0)),
                  pl.BlockSpec((1, d), lambda i: (0, 0)),
                  pl.BlockSpec((1, d), lambda i: (0, 0)),
                  pl.BlockSpec(memory_space=pl.ANY)],
        out_specs=pl.BlockSpec((tt, d), lambda i: (i, 0)),
        out_shape=jax.ShapeDtypeStruct((m, d), F32),
        scratch_shapes=[pltpu.VMEM((2, TOP_K, tt, d), F32), pltpu.SemaphoreType.DMA((2,))],
        compiler_params=_cparams(("arbitrary",)),
        name="combine",
    )(dest_tiles, dest_tiles, x, gates_t, g.reshape(1, d), b.reshape(1, d), ys)


def _moe(x, w_router, b_router, w1, w3, w2, layer, g, b, *, tile):
    m, d = x.shape
    meta, gates, counts = _router(x, w_router, b_router, tm=tile.router)
    counts = counts[:, 0]
    te = tile.expert
    padded = (counts + te - 1) // te * te
    ends = jnp.cumsum(padded)
    starts = ends - padded
    expert_ids = meta[0:TOP_K]
    first_row = sum(jnp.where(expert_ids == e, starts[e], 0) for e in range(N_EXPERTS))
    dest = first_row + meta[TOP_K:2 * TOP_K]
    n_tiles = TOP_K * m // te + N_EXPERTS
    tile_start = jnp.arange(n_tiles, dtype=I32) * te
    tile_expert = jnp.minimum(jnp.sum((tile_start[:, None] >= ends[None, :]).astype(I32), axis=1), N_EXPERTS - 1)
    n_valid = (ends[-1:] // te).astype(I32)

    rows = n_tiles * te
    pad_start = jnp.minimum((starts + counts) // SUBLANES * SUBLANES, rows - te - SUBLANES).astype(I32)
    pad_start = jnp.concatenate([pad_start, n_valid])

    tt = tile.token
    dest_tiles = dest.reshape(TOP_K, m // tt, tt).transpose(1, 0, 2).reshape(m // tt, 1, TOP_K * tt)
    xs = _dispatch(x, dest_tiles, pad_start, rows, tt=tt, te=te)
    ys = _experts(xs, tile_expert, n_valid, w1, w3, w2, layer, tm=te, tf=tile.ff)
    return _combine(x, ys, dest_tiles, gates.T, g, b, tt=tt)


class _Tiles:
    def __init__(self, bsz, seq):
        m = bsz * seq
        self.proj = min(1024, m)
        self.sgu = min(1024, seq)
        self.attn = min(512, seq)
        self.ret = min(512, seq)
        self.ffn = min(1024, m)
        self.ff = 512
        self.router = min(1024, m)
        self.expert = min(1024, m)
        self.token = min(512, m)


def _even_layer(x, p, layer_idx, bsz, seq, tile):
    m, d = x.shape
    q_cols = (2 * MIX_A, 2 * MIX_A + DA_HEADS * 2 * DA_HEAD_DIM)
    h = _matmul(x, p["w_in"], tm=tile.proj, tn=p["w_in"].shape[1],
                scaled_cols=q_cols, scale=DA_HEAD_DIM ** -0.5 * LOG2E)
    a_out = _sgu(h, p["w_s"], p["b_s"], p["v_ln_g"], p["v_ln_b"], tm=tile.sgu)
    lam_init = 0.8 - 0.6 * math.exp(-0.3 * layer_idx)
    lam = (jnp.exp(jnp.sum(p["lam_q1"] * p["lam_k1"])) - jnp.exp(jnp.sum(p["lam_q2"] * p["lam_k2"]))
           + lam_init).reshape(1).astype(F32)
    slopes = 2.0 ** (-8.0 * jnp.arange(1, DA_HEADS + 1, dtype=F32) / DA_HEADS)
    b_out = _diff_attn(h.reshape(bsz, seq, -1), slopes, lam, p["subln_g"], tq=tile.attn, lam_init=lam_init)
    x = _proj_res_ln([a_out, b_out.reshape(m, -1)], p["w_o"], x, p["ln1_g"], p["ln1_b"], tm=tile.proj)
    return _ffn(x, p["ffn_w1"], p["ffn_w3"], p["ffn_w2"], p["ln2_g"], p["ln2_b"], tm=tile.ffn, tf=tile.ff)


def _odd_layer(x, p, bsz, seq, tile):
    m, d = x.shape
    n_in = p["w_in"].shape[1]
    h = _matmul(x, p["w_in"], tm=tile.proj, tn=n_in // 2)
    o = _retention(h.reshape(bsz, seq, n_in), p["gn_g"], tr=tile.ret)
    x = _proj_res_ln([o.reshape(m, -1)], p["w_o"], x, p["ln1_g"], p["ln1_b"], tm=tile.proj)
    return _moe(x, p["router_w"], p["router_b"], p["moe_w1"], p["moe_w3"], p["moe_w2"], p["moe_layer"],
                p["ln2_g"], p["ln2_b"], tile=tile)


def kernel(x, even_w_in, even_w_s, even_b_s, even_v_ln_g, even_v_ln_b, even_lam_q1, even_lam_k1, even_lam_q2, even_lam_k2, even_subln_g, even_w_o, even_ln1_g, even_ln1_b, ffn_w1, ffn_w3, ffn_w2, even_ln2_g, even_ln2_b, odd_w_in, odd_gn_g, odd_w_o, odd_ln1_g, odd_ln1_b, router_w, router_b, moe_w1, moe_w3, moe_w2, odd_ln2_g, odd_ln2_b):
    bsz, seq, d = x.shape
    tile = _Tiles(bsz, seq)
    bf = lambda w: w.astype(BF16)
    moe_w1_bf, moe_w3_bf, moe_w2_bf = bf(moe_w1), bf(moe_w3), bf(moe_w2)
    xf = x.reshape(bsz * seq, d)
    for layer in range(DEPTH):
        i = layer // 2
        if layer % 2 == 0:
            p = dict(w_in=bf(even_w_in[i]), w_s=even_w_s[i], b_s=even_b_s[i], v_ln_g=even_v_ln_g[i],
                     v_ln_b=even_v_ln_b[i], lam_q1=even_lam_q1[i], lam_k1=even_lam_k1[i],
                     lam_q2=even_lam_q2[i], lam_k2=even_lam_k2[i], subln_g=even_subln_g[i],
                     w_o=bf(even_w_o[i]), ln1_g=even_ln1_g[i], ln1_b=even_ln1_b[i],
                     ffn_w1=bf(ffn_w1[i]), ffn_w3=bf(ffn_w3[i]), ffn_w2=bf(ffn_w2[i]),
                     ln2_g=even_ln2_g[i], ln2_b=even_ln2_b[i])
            xf = _even_layer(xf, p, layer, bsz, seq, tile)
        else:
            p = dict(w_in=bf(odd_w_in[i]), gn_g=odd_gn_g[i], w_o=bf(odd_w_o[i]),
                     ln1_g=odd_ln1_g[i], ln1_b=odd_ln1_b[i], router_w=router_w[i], router_b=router_b[i],
                     moe_w1=moe_w1_bf, moe_w3=moe_w3_bf, moe_w2=moe_w2_bf, moe_layer=i,
                     ln2_g=odd_ln2_g[i], ln2_b=odd_ln2_b[i])
            xf = _odd_layer(xf, p, bsz, seq, tile)
    return xf.reshape(bsz, seq, d)
```

```python
import functools
import math

import jax
import jax.numpy as jnp
from jax import lax
from jax.experimental import pallas as pl
from jax.experimental.pallas import tpu as pltpu

F32 = jnp.float32
BF16 = jnp.bfloat16
I32 = jnp.int32

D_MODEL = 1024
DEPTH = 4
DN_ALPHA = (2.0 * DEPTH) ** 0.25
LN_EPS = 1e-5

MIX_A = D_MODEL // 2
GM_GROUPS = 4
GM_DIM = MIX_A // GM_GROUPS
GM_CHUNK = 128
DA_HEADS = 4
DA_HEAD_DIM = 64
DA_V_DIM = 128
RET_HEADS = 4
RET_DK = D_MODEL // RET_HEADS
RET_DV = 2 * D_MODEL // RET_HEADS
RET_CHUNK = 256
D_FF = 7 * D_MODEL // 2
N_EXPERTS = 8
TOP_K = 2

V7X_VMEM_LIMIT = 56 * 1024 * 1024
SUBLANES = 8
NEG_BIG = -1e30
LOG2E = 1.4426950408889634

NT_DIMS = (((1,), (1,)), ((), ()))
TN_DIMS = (((0,), (0,)), ((), ()))


def _cparams(sem):
    return pltpu.CompilerParams(dimension_semantics=sem, vmem_limit_bytes=V7X_VMEM_LIMIT)


def _layer_norm(z, g, b):
    mu = jnp.mean(z, axis=-1, keepdims=True)
    zc = z - mu
    var = jnp.mean(zc * zc, axis=-1, keepdims=True)
    return zc * lax.rsqrt(var + LN_EPS) * g + b


def _gelu(x):
    return 0.5 * x * (1.0 + lax.erf(x * (2.0 ** -0.5)))


def _silu(x):
    return x * jax.nn.sigmoid(x)


def _matmul_kernel(x_ref, w_ref, o_ref, *, col_chunk):
    xb = x_ref[...].astype(BF16)
    for c0 in range(0, o_ref.shape[1], col_chunk):
        o_ref[:, c0:c0 + col_chunk] = jnp.dot(
            xb, w_ref[:, c0:c0 + col_chunk], preferred_element_type=F32).astype(o_ref.dtype)


def _matmul(x, w, *, tm, tn):
    m, k = x.shape
    n = w.shape[1]
    return pl.pallas_call(
        functools.partial(_matmul_kernel, col_chunk=512),
        grid=(n // tn, m // tm),
        in_specs=[pl.BlockSpec((tm, k), lambda j, i: (i, 0)),
                  pl.BlockSpec((k, tn), lambda j, i: (0, j))],
        out_specs=pl.BlockSpec((tm, tn), lambda j, i: (i, j)),
        out_shape=jax.ShapeDtypeStruct((m, n), BF16),
        compiler_params=_cparams(("parallel", "parallel")),
        name="in_proj",
    )(x, w)


def _proj_res_ln_kernel(*refs, n_lhs, row_chunk):
    a_refs = refs[:n_lhs]
    w_ref, r_ref, g_ref, b_ref, o_ref = refs[n_lhs:]
    for r0 in range(0, r_ref.shape[0], row_chunk):
        rows = slice(r0, r0 + row_chunk)
        y = None
        k0 = 0
        for a_ref in a_refs:
            kk = a_ref.shape[1]
            part = jnp.dot(a_ref[rows, :], w_ref[k0:k0 + kk, :], preferred_element_type=F32)
            y = part if y is None else y + part
            k0 += kk
        o_ref[rows, :] = _layer_norm(DN_ALPHA * r_ref[rows, :] + y, g_ref[...], b_ref[...])


def _proj_res_ln(a_list, w, res, g, b, *, tm):
    m, d = res.shape
    in_specs = [pl.BlockSpec((tm, a.shape[1]), lambda i: (i, 0)) for a in a_list]
    in_specs += [pl.BlockSpec(w.shape, lambda i: (0, 0), pipeline_mode=pl.Buffered(1)),
                 pl.BlockSpec((tm, d), lambda i: (i, 0)),
                 pl.BlockSpec((1, d), lambda i: (0, 0)),
                 pl.BlockSpec((1, d), lambda i: (0, 0))]
    return pl.pallas_call(
        functools.partial(_proj_res_ln_kernel, n_lhs=len(a_list), row_chunk=min(256, tm)),
        grid=(m // tm,),
        in_specs=in_specs,
        out_specs=pl.BlockSpec((tm, d), lambda i: (i, 0)),
        out_shape=jax.ShapeDtypeStruct((m, d), F32),
        compiler_params=_cparams(("parallel",)),
        name="out_proj_ln",
    )(*a_list, w, res, g.reshape(1, d), b.reshape(1, d))


def _in_proj_sgu_kernel(x_ref, w_ref, ws_ref, bst_ref, lg_ref, lb_ref, a_ref, h_ref, *, col_chunk, q_cols, q_scale):
    tm = x_ref.shape[0]
    xb = x_ref[...].astype(BF16)
    u_all = jnp.dot(xb, w_ref[:, 0:MIX_A], preferred_element_type=F32)
    v_all = jnp.dot(xb, w_ref[:, MIX_A:2 * MIX_A], preferred_element_type=F32)
    for c0 in range(2 * MIX_A, w_ref.shape[1], col_chunk):
        y = jnp.dot(xb, w_ref[:, c0:c0 + col_chunk], preferred_element_type=F32)
        if q_cols[0] <= c0 < q_cols[1]:
            y = y * q_scale
        h_ref[:, c0 - 2 * MIX_A:c0 - 2 * MIX_A + col_chunk] = y.astype(h_ref.dtype)

    row = lax.broadcasted_iota(I32, (GM_CHUNK, GM_CHUNK), 0)
    col = lax.broadcasted_iota(I32, (GM_CHUNK, GM_CHUNK), 1)
    causal = row >= col
    for g in range(GM_GROUPS):
        w = jnp.where(causal, ws_ref[g], 0.0).astype(BF16)
        bias = bst_ref[:, g:g + 1]
        cols = slice(g * GM_DIM, (g + 1) * GM_DIM)
        for r0 in range(0, tm, GM_CHUNK):
            rows = slice(r0, r0 + GM_CHUNK)
            vn = _layer_norm(_gelu(v_all[rows, cols]), lg_ref[g:g + 1, :], lb_ref[g:g + 1, :])
            mixed = jnp.dot(w, vn.astype(BF16), preferred_element_type=F32) + bias
            a_ref[rows, cols] = (_gelu(u_all[rows, cols]) * mixed).astype(a_ref.dtype)


def _in_proj_sgu(x, w, w_s, b_s, ln_g, ln_b, *, tm, q_scale):
    m, k = x.shape
    n = w.shape[1]
    col_chunk = 512
    q_cols = (2 * MIX_A, 2 * MIX_A + DA_HEADS * 2 * DA_HEAD_DIM)
    assert q_cols[0] % col_chunk == 0 and q_cols[1] % col_chunk == 0
    return pl.pallas_call(
        functools.partial(_in_proj_sgu_kernel, col_chunk=col_chunk, q_cols=q_cols, q_scale=q_scale),
        grid=(m // tm,),
        in_specs=[pl.BlockSpec((tm, k), lambda i: (i, 0)),
                  pl.BlockSpec((k, n), lambda i: (0, 0), pipeline_mode=pl.Buffered(1)),
                  pl.BlockSpec(w_s.shape, lambda i: (0, 0, 0)),
                  pl.BlockSpec((GM_CHUNK, GM_GROUPS), lambda i: (0, 0)),
                  pl.BlockSpec(ln_g.shape, lambda i: (0, 0)),
                  pl.BlockSpec(ln_b.shape, lambda i: (0, 0))],
        out_specs=[pl.BlockSpec((tm, MIX_A), lambda i: (i, 0)),
                   pl.BlockSpec((tm, n - 2 * MIX_A), lambda i: (i, 0))],
        out_shape=[jax.ShapeDtypeStruct((m, MIX_A), BF16),
                   jax.ShapeDtypeStruct((m, n - 2 * MIX_A), BF16)],
        compiler_params=_cparams(("parallel",)),
        name="in_proj_sgu",
    )(x, w, w_s, b_s.T, ln_g, ln_b)


def _alibi_features(slopes2, tq):
    rest = slopes2
    pieces = []
    for _ in range(3):
        piece = rest.astype(BF16).astype(F32)
        pieces.append(piece)
        rest = rest - piece
    pos = jnp.arange(tq, dtype=I32)
    hi = jnp.broadcast_to(((pos >> 4) << 4).astype(F32), (DA_HEADS, tq))
    lo = jnp.broadcast_to((pos & 15).astype(F32), (DA_HEADS, tq))
    qf = jnp.zeros((DA_HEADS, tq, DA_V_DIM), F32)
    kf = jnp.zeros((DA_HEADS, tq, DA_V_DIM), F32)
    for t, piece in enumerate(pieces):
        const = jnp.broadcast_to(piece[:, None], (DA_HEADS, tq))
        a, b = 2 * t, 2 * t + 1
        qf = qf.at[:, :, a].set(hi).at[:, :, b].set(lo)
        kf = kf.at[:, :, a].set(-const).at[:, :, b].set(-const)
        a, b = 6 + 2 * t, 7 + 2 * t
        qf = qf.at[:, :, a].set(const).at[:, :, b].set(const)
        kf = kf.at[:, :, a].set(hi).at[:, :, b].set(lo)
    return qf.astype(BF16), kf.astype(BF16)


def _diff_attn_kernel(sl_ref, lam_ref, q_ref, k_ref, v_ref, qf_ref, kf_ref, g_ref, o_ref,
                      sa_ref, sb_ref, acc_ref, m_ref, vt_ref, *, tq, lam_init):
    h = pl.program_id(1)
    qi = pl.program_id(2)
    slope2 = sl_ref[h]
    lam = lam_ref[0]

    @pl.when(qi == 0)
    def _():
        vt_ref[0:DA_V_DIM, :] = jnp.transpose(v_ref[0].astype(F32)).astype(BF16)
        sub = lax.broadcasted_iota(I32, (DA_V_DIM, vt_ref.shape[1]), 0)
        vt_ref[DA_V_DIM:2 * DA_V_DIM, :] = jnp.where(sub == 0, 1.0, 0.0).astype(BF16)

    lane = lax.broadcasted_iota(I32, (tq, DA_V_DIM), 1)
    q = q_ref[0]
    zero = jnp.zeros_like(q)
    q_maps = tuple(jnp.concatenate([jnp.where(keep, q, zero), qf_ref[0]], axis=1)
                   for keep in (lane < DA_HEAD_DIM, lane >= DA_HEAD_DIM))
    kf = kf_ref[0]

    row = lax.broadcasted_iota(I32, (tq, tq), 0)
    col = lax.broadcasted_iota(I32, (tq, tq), 1)

    def scores(j, s_ref):
        start = pl.multiple_of(j * tq, tq)
        kj = jnp.concatenate([k_ref[0, pl.ds(start, tq), :], kf], axis=1)
        for c in range(2):
            s_ref[c] = lax.dot_general(kj, q_maps[c], NT_DIMS, preferred_element_type=F32)

    def update(j, s_ref, diagonal):
        start = pl.multiple_of(j * tq, tq)
        vt = vt_ref[:, pl.ds(start, tq)]
        off = -slope2 * ((qi - j) * tq).astype(F32)
        for c in range(2):
            s = s_ref[c]
            if diagonal:
                s = jnp.where(col >= row, s, NEG_BIG)
            m = m_ref[c]
            m_new = jnp.maximum(m, jnp.max(s, axis=0, keepdims=True) + off)
            p = jnp.exp2(s - (m_new - off)).astype(BF16)
            acc_ref[c] = jnp.exp2(m - m_new) * acc_ref[c] + jnp.dot(vt, p, preferred_element_type=F32)
            m_ref[c] = m_new

    m_ref[...] = jnp.full(m_ref.shape, NEG_BIG, F32)
    acc_ref[...] = jnp.zeros_like(acc_ref)

    scores(0, sa_ref)

    def pair(t, carry):
        b = 2 * t
        scores(b + 1, sb_ref)
        update(b, sa_ref, False)
        scores(b + 2, sa_ref)
        update(b + 1, sb_ref, False)
        return carry

    lax.fori_loop(0, qi // 2, pair, 0)

    @pl.when(qi % 2 == 1)
    def _():
        scores(qi, sb_ref)
        update(qi - 1, sa_ref, False)
        update(qi, sb_ref, True)

    @pl.when(qi % 2 == 0)
    def _():
        update(qi, sa_ref, True)

    acc_a = acc_ref[0]
    acc_b = acc_ref[1]
    o = (acc_a[:DA_V_DIM, :] / acc_a[DA_V_DIM:DA_V_DIM + 1, :]
         - lam * (acc_b[:DA_V_DIM, :] / acc_b[DA_V_DIM:DA_V_DIM + 1, :]))
    o = o * lax.rsqrt(jnp.mean(o * o, axis=0, keepdims=True) + LN_EPS) * g_ref[...]
    o_ref[0] = (jnp.transpose(o) * (1.0 - lam_init)).astype(o_ref.dtype)


def _diff_attn(h3, slopes, lam, subln_g, *, tq, lam_init):
    bsz, s, _ = h3.shape
    slopes2 = slopes * LOG2E
    qf, kf = _alibi_features(slopes2, tq)
    blk = DA_V_DIM
    q_blk0 = 0
    k_blk0 = q_blk0 + DA_HEADS
    v_blk0 = k_blk0 + DA_HEADS
    smem = pl.BlockSpec(memory_space=pltpu.SMEM)
    return pl.pallas_call(
        functools.partial(_diff_attn_kernel, tq=tq, lam_init=lam_init),
        grid=(bsz, DA_HEADS, s // tq),
        in_specs=[smem, smem,
                  pl.BlockSpec((1, tq, blk), lambda b, h, i: (b, i, q_blk0 + h)),
                  pl.BlockSpec((1, s, blk), lambda b, h, i: (b, 0, k_blk0 + h)),
                  pl.BlockSpec((1, s, blk), lambda b, h, i: (b, 0, v_blk0 + h)),
                  pl.BlockSpec((1, tq, blk), lambda b, h, i: (h, 0, 0)),
                  pl.BlockSpec((1, tq, blk), lambda b, h, i: (h, 0, 0)),
                  pl.BlockSpec((blk, 1), lambda b, h, i: (0, 0))],
        out_specs=pl.BlockSpec((1, tq, blk), lambda b, h, i: (b, i, h)),
        out_shape=jax.ShapeDtypeStruct((bsz, s, DA_HEADS * DA_V_DIM), BF16),
        scratch_shapes=[pltpu.VMEM((2, tq, tq), F32), pltpu.VMEM((2, tq, tq), F32),
                        pltpu.VMEM((2, 2 * DA_V_DIM, tq), F32), pltpu.VMEM((2, 1, tq), F32),
                        pltpu.VMEM((2 * DA_V_DIM, s), BF16)],
        compiler_params=_cparams(("parallel", "parallel", "arbitrary")),
        name="diff_attn",
    )(slopes2, lam, h3, h3, h3, qf, kf, subln_g.reshape(blk, 1))


def _retention_kernel(q_ref, k_ref, v_ref, g_ref, gn_ref, o_ref, state_ref):
    tr = q_ref.shape[1]

    @pl.when(pl.program_id(1) == 0)
    def _():
        state_ref[...] = jnp.zeros_like(state_ref)

    idx_r = lax.broadcasted_iota(I32, (RET_CHUNK, RET_CHUNK), 0)
    idx_c = lax.broadcasted_iota(I32, (RET_CHUNK, RET_CHUNK), 1)
    rel = (idx_r - idx_c).astype(F32)
    pos = lax.broadcasted_iota(I32, (RET_CHUNK, 1), 0).astype(F32)
    for h in range(RET_HEADS):
        log_g = math.log1p(-(2.0 ** (-5.0 - h)))
        inner_decay = jnp.where(rel >= 0, jnp.exp(log_g * jnp.maximum(rel, 0.0)), 0.0)
        q_decay = jnp.exp(log_g * (pos + 1.0))
        k_decay = jnp.exp(log_g * (RET_CHUNK - 1.0 - pos))
        chunk_decay = math.exp(log_g * RET_CHUNK)
        kc = slice(h * RET_DK, (h + 1) * RET_DK)
        vc = slice(h * RET_DV, (h + 1) * RET_DV)
        for c0 in range(0, tr, RET_CHUNK):
            rows = slice(c0, c0 + RET_CHUNK)
            q = q_ref[0, rows, kc]
            k = k_ref[0, rows, kc] * (RET_DK ** -0.5)
            v = v_ref[0, rows, vc]
            state = state_ref[h]
            scores = lax.dot_general(q, k, NT_DIMS, preferred_element_type=F32) * inner_decay
            inner = jnp.dot(scores.astype(BF16), v, preferred_element_type=F32)
            cross = jnp.dot(q, state.astype(BF16), preferred_element_type=F32) * q_decay
            k_dec = (k.astype(F32) * k_decay).astype(BF16)
            state_ref[h] = state * chunk_decay + lax.dot_general(
                k_dec, v, TN_DIMS, preferred_element_type=F32)
            o = inner + cross
            mu = jnp.mean(o, axis=-1, keepdims=True)
            oc = o - mu
            var = jnp.mean(oc * oc, axis=-1, keepdims=True)
            on = oc * lax.rsqrt(var + LN_EPS) * gn_ref[h:h + 1, :]
            gate = g_ref[0, rows, vc].astype(F32)
            o_ref[0, rows, vc] = (_silu(gate) * on).astype(o_ref.dtype)


def _retention(h3, gn_g, *, tr):
    bsz, s, _ = h3.shape
    dq = RET_HEADS * RET_DK
    dv = RET_HEADS * RET_DV
    return pl.pallas_call(
        _retention_kernel,
        grid=(bsz, s // tr),
        in_specs=[pl.BlockSpec((1, tr, dq), lambda b, i: (b, i, 0)),
                  pl.BlockSpec((1, tr, dq), lambda b, i: (b, i, 1)),
                  pl.BlockSpec((1, tr, dv), lambda b, i: (b, i, 1)),
                  pl.BlockSpec((1, tr, dv), lambda b, i: (b, i, 2)),
                  pl.BlockSpec(gn_g.shape, lambda b, i: (0, 0))],
        out_specs=pl.BlockSpec((1, tr, dv), lambda b, i: (b, i, 0)),
        out_shape=jax.ShapeDtypeStruct((bsz, s, dv), BF16),
        scratch_shapes=[pltpu.VMEM((RET_HEADS, RET_DK, RET_DV), F32)],
        compiler_params=_cparams(("parallel", "arbitrary")),
        name="retention",
    )(h3, h3, h3, h3, gn_g)


def _swiglu(xb, w1, w3, w2, tf):
    acc = None
    for f0 in range(0, w2.shape[0], tf):
        h1 = jnp.dot(xb, w1[:, f0:f0 + tf], preferred_element_type=F32)
        h3 = jnp.dot(xb, w3[:, f0:f0 + tf], preferred_element_type=F32)
        part = jnp.dot((_silu(h1) * h3).astype(BF16), w2[f0:f0 + tf, :], preferred_element_type=F32)
        acc = part if acc is None else acc + part
    return acc


def _ffn_kernel(x_ref, w1_ref, w3_ref, w2_ref, g_ref, b_ref, o_ref, *, tf, row_chunk):
    for r0 in range(0, x_ref.shape[0], row_chunk):
        rows = slice(r0, r0 + row_chunk)
        x = x_ref[rows, :]
        y = _swiglu(x.astype(BF16), w1_ref, w3_ref, w2_ref, tf)
        o_ref[rows, :] = _layer_norm(DN_ALPHA * x + y, g_ref[...], b_ref[...])


def _ffn(x, w1, w3, w2, g, b, *, tm, tf):
    m, d = x.shape
    resident = dict(pipeline_mode=pl.Buffered(1))
    return pl.pallas_call(
        functools.partial(_ffn_kernel, tf=tf, row_chunk=min(512, tm)),
        grid=(m // tm,),
        in_specs=[pl.BlockSpec((tm, d), lambda i: (i, 0)),
                  pl.BlockSpec(w1.shape, lambda i: (0, 0), **resident),
                  pl.BlockSpec(w3.shape, lambda i: (0, 0), **resident),
                  pl.BlockSpec(w2.shape, lambda i: (0, 0), **resident),
                  pl.BlockSpec((1, d), lambda i: (0, 0)),
                  pl.BlockSpec((1, d), lambda i: (0, 0))],
        out_specs=pl.BlockSpec((tm, d), lambda i: (i, 0)),
        out_shape=jax.ShapeDtypeStruct((m, d), F32),
        compiler_params=_cparams(("parallel",)),
        name="ffn",
    )(x, w1, w3, w2, g.reshape(1, d), b.reshape(1, d))


def _experts_kernel(te_ref, nv_ref, xs_ref, w1_ref, w3_ref, w2_ref, ys_ref, *, tf):
    del te_ref
    i = pl.program_id(0)

    @pl.when(i < nv_ref[0])
    def _():
        ys_ref[...] = _swiglu(xs_ref[...].astype(BF16), w1_ref.at[0, 0], w3_ref.at[0, 0], w2_ref.at[0, 0], tf)

    @pl.when(i >= nv_ref[0])
    def _():
        ys_ref[...] = jnp.zeros_like(ys_ref)


def _experts(xs, tile_expert, n_valid, w1, w3, w2, layer, *, tm, tf):
    r, d = xs.shape
    resident = dict(pipeline_mode=pl.Buffered(1))

    def expert_block(w):
        return pl.BlockSpec((1, 1) + w.shape[2:], lambda i, te, nv: (layer, te[i], 0, 0), **resident)

    grid_spec = pltpu.PrefetchScalarGridSpec(
        num_scalar_prefetch=2,
        grid=(r // tm,),
        in_specs=[pl.BlockSpec((tm, d), lambda i, te, nv: (jnp.minimum(i, nv[0] - 1), 0)),
                  expert_block(w1), expert_block(w3), expert_block(w2)],
        out_specs=pl.BlockSpec((tm, d), lambda i, te, nv: (i, 0)),
    )
    return pl.pallas_call(
        functools.partial(_experts_kernel, tf=tf),
        grid_spec=grid_spec,
        out_shape=jax.ShapeDtypeStruct((r, d), F32),
        compiler_params=_cparams(("arbitrary",)),
        name="experts",
    )(tile_expert, n_valid, xs, w1, w3, w2)


def _split_bf16(x):
    hi = x.astype(BF16)
    return hi, (x - hi.astype(F32)).astype(BF16)


def _router_kernel(x_ref, wt_ref, b_ref, meta_ref, gate_ref, cnt_ref, carry_ref, tri_ref):
    i = pl.program_id(0)
    tm = x_ref.shape[0]

    @pl.when(i == 0)
    def _():
        carry_ref[...] = jnp.zeros_like(carry_ref)
        a = lax.broadcasted_iota(I32, (tm, tm), 0)
        b = lax.broadcasted_iota(I32, (tm, tm), 1)
        tri_ref[...] = (a <= b).astype(BF16)

    xh, xl = _split_bf16(x_ref[...])
    wh, wl = _split_bf16(wt_ref[...])
    logits = (lax.dot_general(wh, xh, NT_DIMS, preferred_element_type=F32)
              + lax.dot_general(wh, xl, NT_DIMS, preferred_element_type=F32)
              + lax.dot_general(wl, xh, NT_DIMS, preferred_element_type=F32)
              + b_ref[...])

    e_iota = lax.broadcasted_iota(I32, logits.shape, 0)
    m1 = jnp.max(logits, axis=0, keepdims=True)
    i1 = jnp.min(jnp.where(logits == m1, e_iota, N_EXPERTS), axis=0, keepdims=True)
    rest = jnp.where(e_iota == i1, -jnp.inf, logits)
    m2 = jnp.max(rest, axis=0, keepdims=True)
    i2 = jnp.min(jnp.where(rest == m2, e_iota, N_EXPERTS), axis=0, keepdims=True)
    e2 = jnp.exp(m2 - m1)
    gate_ref[0:1, :] = 1.0 / (1.0 + e2)
    gate_ref[1:2, :] = e2 / (1.0 + e2)

    pick1 = e_iota == i1
    pick2 = e_iota == i2
    chosen = jnp.where(pick1 | pick2, 1.0, 0.0)
    incl = jnp.dot(chosen.astype(BF16), tri_ref[...], preferred_element_type=F32)
    before = carry_ref[...] + incl - chosen
    meta_ref[0:1, :] = i1
    meta_ref[1:2, :] = i2
    meta_ref[2:3, :] = jnp.sum(jnp.where(pick1, before, 0.0), axis=0, keepdims=True).astype(I32)
    meta_ref[3:4, :] = jnp.sum(jnp.where(pick2, before, 0.0), axis=0, keepdims=True).astype(I32)
    carry_ref[...] += incl[:, tm - 1:tm]
    cnt_ref[...] = jnp.broadcast_to(carry_ref[...], cnt_ref.shape).astype(I32)


def _router(x, w_router, b_router, *, tm):
    m, d = x.shape
    return pl.pallas_call(
        _router_kernel,
        grid=(m // tm,),
        in_specs=[pl.BlockSpec((tm, d), lambda i: (i, 0)),
                  pl.BlockSpec((N_EXPERTS, d), lambda i: (0, 0)),
                  pl.BlockSpec((N_EXPERTS, 1), lambda i: (0, 0))],
        out_specs=[pl.BlockSpec((4, tm), lambda i: (0, i)),
                   pl.BlockSpec((TOP_K, tm), lambda i: (0, i)),
                   pl.BlockSpec((N_EXPERTS, 128), lambda i: (0, 0))],
        out_shape=[jax.ShapeDtypeStruct((4, m), I32),
                   jax.ShapeDtypeStruct((TOP_K, m), F32),
                   jax.ShapeDtypeStruct((N_EXPERTS, 128), I32)],
        scratch_shapes=[pltpu.VMEM((N_EXPERTS, 1), F32), pltpu.VMEM((tm, tm), BF16)],
        compiler_params=_cparams(("arbitrary",)),
        name="router",
    )(x, w_router.T, b_router.reshape(N_EXPERTS, 1))


ROW_DMA_UNROLL = 8


def _for_each_row(tt, fn):
    def trip(t, carry):
        for k in range(ROW_DMA_UNROLL):
            fn(t * ROW_DMA_UNROLL + k, k)
        return carry

    lax.fori_loop(0, tt // ROW_DMA_UNROLL, trip, 0)


def _dispatch_kernel(dest_ref, dest_prev_ref, pad_ref, x_ref, xs_ref, stage_ref, zero_ref, sem, zero_sem, *, tt):
    i = pl.program_id(0)
    slot = i % 2

    def zero_copy(e):
        start = pl.multiple_of(pad_ref[e], SUBLANES)
        return pltpu.make_async_copy(zero_ref, xs_ref.at[pl.ds(start, zero_ref.shape[0]), :], zero_sem)

    te = zero_ref.shape[0] - SUBLANES
    n_tiles = xs_ref.shape[0] // te

    def tail_copy(t):
        return pltpu.make_async_copy(zero_ref.at[pl.ds(0, te), :],
                                     xs_ref.at[pl.ds(pl.multiple_of(t * te, te), te), :], zero_sem)

    @pl.when(i == 0)
    def _():
        zero_ref[...] = jnp.zeros_like(zero_ref)
        for e in range(N_EXPERTS):
            zero_copy(e).start()
            zero_copy(e).wait()
        for e in range(N_EXPERTS):
            t = pad_ref[N_EXPERTS] + e

            @pl.when(t < n_tiles)
            def _():
                tail_copy(t).start()
                tail_copy(t).wait()

    def copy(idx_ref, r, j, s):
        return pltpu.make_async_copy(stage_ref.at[s, pl.ds(r, 1), :],
                                     xs_ref.at[pl.ds(idx_ref[0, 0, j * tt + r], 1), :], sem.at[s])

    def wait_all(idx_ref, s):
        _for_each_row(tt, lambda r, k: [copy(idx_ref, r, j, s).wait() for j in range(TOP_K)])

    stage_ref[slot] = x_ref[...]
    _for_each_row(tt, lambda r, k: [copy(dest_ref, r, j, slot).start() for j in range(TOP_K)])

    @pl.when(i > 0)
    def _():
        wait_all(dest_prev_ref, 1 - slot)

    @pl.when(i == pl.num_programs(0) - 1)
    def _():
        wait_all(dest_ref, slot)


def _dispatch(x, dest_tiles, pad_start, rows, *, tt, te):
    m, d = x.shape
    idx_block = (1, 1, TOP_K * tt)
    return pl.pallas_call(
        functools.partial(_dispatch_kernel, tt=tt),
        grid=(m // tt,),
        in_specs=[pl.BlockSpec(idx_block, lambda i: (i, 0, 0), memory_space=pltpu.SMEM),
                  pl.BlockSpec(idx_block, lambda i: (jnp.maximum(i - 1, 0), 0, 0), memory_space=pltpu.SMEM),
                  pl.BlockSpec(memory_space=pltpu.SMEM),
                  pl.BlockSpec((tt, d), lambda i: (i, 0))],
        out_specs=pl.BlockSpec(memory_space=pl.ANY),
        out_shape=jax.ShapeDtypeStruct((rows, d), F32),
        scratch_shapes=[pltpu.VMEM((2, tt, d), F32), pltpu.VMEM((te + SUBLANES, d), F32),
                        pltpu.SemaphoreType.DMA((2,)), pltpu.SemaphoreType.DMA(())],
        compiler_params=_cparams(("arbitrary",)),
        name="dispatch",
    )(dest_tiles, dest_tiles, pad_start, x)


def _combine_kernel(dest_ref, dest_next_ref, x_ref, gate_ref, g_ref, b_ref, ys_ref, o_ref, buf_ref, sem, *, tt):
    i = pl.program_id(0)
    slot = i % 2

    def copy(idx_ref, r, j, s):
        return pltpu.make_async_copy(ys_ref.at[pl.ds(idx_ref[0, 0, j * tt + r], 1), :],
                                     buf_ref.at[s, j, pl.ds(r, 1), :], sem.at[s])

    def start_all(idx_ref, s):
        _for_each_row(tt, lambda r, k: [copy(idx_ref, r, j, s).start(priority=(k + j) % 2) for j in range(TOP_K)])

    @pl.when(i == 0)
    def _():
        start_all(dest_ref, slot)

    @pl.when(i + 1 < pl.num_programs(0))
    def _():
        start_all(dest_next_ref, 1 - slot)

    _for_each_row(tt, lambda r, k: [copy(dest_ref, r, j, slot).wait() for j in range(TOP_K)])

    y = gate_ref[:, 0:1] * buf_ref[slot, 0] + gate_ref[:, 1:2] * buf_ref[slot, 1]
    o_ref[...] = _layer_norm(DN_ALPHA * x_ref[...] + y, g_ref[...], b_ref[...])


def _combine(x, ys, dest_tiles, gates_t, g, b, *, tt):
    m, d = x.shape
    n = m // tt
    idx_block = (1, 1, TOP_K * tt)
    return pl.pallas_call(
        functools.partial(_combine_kernel, tt=tt),
        grid=(n,),
        in_specs=[pl.BlockSpec(idx_block, lambda i: (i, 0, 0), memory_space=pltpu.SMEM),
                  pl.BlockSpec(idx_block, lambda i: (jnp.minimum(i + 1, n - 1), 0, 0), memory_space=pltpu.SMEM),
                  pl.BlockSpec((tt, d), lambda i: (i, 0)),
                  pl.BlockSpec((tt, TOP_K), lambda i: (i, 0)),
                  pl.BlockSpec((1, d), lambda i: (0, 0)),
                  pl.BlockSpec((1, d), lambda i: (0, 0)),
                  pl.BlockSpec(memory_space=pl.ANY)],
        out_specs=pl.BlockSpec((tt, d), lambda i: (i, 0)),
        out_shape=jax.ShapeDtypeStruct((m, d), F32),
        scratch_shapes=[pltpu.VMEM((2, TOP_K, tt, d), F32), pltpu.SemaphoreType.DMA((2,))],
        compiler_params=_cparams(("arbitrary",)),
        name="combine",
    )(dest_tiles, dest_tiles, x, gates_t, g.reshape(1, d), b.reshape(1, d), ys)


def _moe(x, w_router, b_router, w1, w3, w2, layer, g, b, *, tile):
    m, d = x.shape
    meta, gates, counts = _router(x, w_router, b_router, tm=tile.router)
    counts = counts[:, 0]
    te = tile.expert
    padded = (counts + te - 1) // te * te
    ends = jnp.cumsum(padded)
    starts = ends - padded
    expert_ids = meta[0:TOP_K]
    first_row = sum(jnp.where(expert_ids == e, starts[e], 0) for e in range(N_EXPERTS))
    dest = first_row + meta[TOP_K:2 * TOP_K]
    n_tiles = TOP_K * m // te + N_EXPERTS
    tile_start = jnp.arange(n_tiles, dtype=I32) * te
    tile_expert = jnp.minimum(jnp.sum((tile_start[:, None] >= ends[None, :]).astype(I32), axis=1), N_EXPERTS - 1)
    n_valid = (ends[-1:] // te).astype(I32)

    rows = n_tiles * te
    pad_start = jnp.minimum((starts + counts) // SUBLANES * SUBLANES, rows - te - SUBLANES).astype(I32)
    pad_start = jnp.concatenate([pad_start, n_valid])

    tt = tile.token
    dest_tiles = dest.reshape(TOP_K, m // tt, tt).transpose(1, 0, 2).reshape(m // tt, 1, TOP_K * tt)
    xs = _dispatch(x, dest_tiles, pad_start, rows, tt=tt, te=te)
    ys = _experts(xs, tile_expert, n_valid, w1, w3, w2, layer, tm=te, tf=tile.ff)
    return _combine(x, ys, dest_tiles, gates.T, g, b, tt=tt)


class _Tiles:
    def __init__(self, bsz, seq):
        m = bsz * seq
        self.proj = min(1024, m)
        self.attn = min(512, seq)
        self.ret = min(512, seq)
        self.ffn = min(1024, m)
        self.ff = 512
        self.router = min(1024, m)
        self.expert = min(1024, m)
        self.token = min(512, m)


def _even_layer(x, p, layer_idx, bsz, seq, tile):
    m, d = x.shape
    a_out, h = _in_proj_sgu(x, p["w_in"], p["w_s"], p["b_s"], p["v_ln_g"], p["v_ln_b"],
                            tm=tile.proj, q_scale=DA_HEAD_DIM ** -0.5 * LOG2E)
    lam_init = 0.8 - 0.6 * math.exp(-0.3 * layer_idx)
    lam = (jnp.exp(jnp.sum(p["lam_q1"] * p["lam_k1"])) - jnp.exp(jnp.sum(p["lam_q2"] * p["lam_k2"]))
           + lam_init).reshape(1).astype(F32)
    slopes = 2.0 ** (-8.0 * jnp.arange(1, DA_HEADS + 1, dtype=F32) / DA_HEADS)
    b_out = _diff_attn(h.reshape(bsz, seq, -1), slopes, lam, p["subln_g"], tq=tile.attn, lam_init=lam_init)
    x = _proj_res_ln([a_out, b_out.reshape(m, -1)], p["w_o"], x, p["ln1_g"], p["ln1_b"], tm=tile.proj)
    return _ffn(x, p["ffn_w1"], p["ffn_w3"], p["ffn_w2"], p["ln2_g"], p["ln2_b"], tm=tile.ffn, tf=tile.ff)


def _odd_layer(x, p, bsz, seq, tile):
    m, d = x.shape
    n_in = p["w_in"].shape[1]
    h = _matmul(x, p["w_in"], tm=tile.proj, tn=n_in // 2)
    o = _retention(h.reshape(bsz, seq, n_in), p["gn_g"], tr=tile.ret)
    x = _proj_res_ln([o.reshape(m, -1)], p["w_o"], x, p["ln1_g"], p["ln1_b"], tm=tile.proj)
    return _moe(x, p["router_w"], p["router_b"], p["moe_w1"], p["moe_w3"], p["moe_w2"], p["moe_layer"],
                p["ln2_g"], p["ln2_b"], tile=tile)


def kernel(x, even_w_in, even_w_s, even_b_s, even_v_ln_g, even_v_ln_b, even_lam_q1, even_lam_k1, even_lam_q2, even_lam_k2, even_subln_g, even_w_o, even_ln1_g, even_ln1_b, ffn_w1, ffn_w3, ffn_w2, even_ln2_g, even_ln2_b, odd_w_in, odd_gn_g, odd_w_o, odd_ln1_g, odd_ln1_b, router_w, router_b, moe_w1, moe_w3, moe_w2, odd_ln2_g, odd_ln2_b):
    bsz, seq, d = x.shape
    tile = _Tiles(bsz, seq)
    bf = lambda w: w.astype(BF16)
    moe_w1_bf, moe_w3_bf, moe_w2_bf = bf(moe_w1), bf(moe_w3), bf(moe_w2)
    xf = x.reshape(bsz * seq, d)
    for layer in range(DEPTH):
        i = layer // 2
        if layer % 2 == 0:
            p = dict(w_in=bf(even_w_in[i]), w_s=even_w_s[i], b_s=even_b_s[i], v_ln_g=even_v_ln_g[i],
                     v_ln_b=even_v_ln_b[i], lam_q1=even_lam_q1[i], lam_k1=even_lam_k1[i],
                     lam_q2=even_lam_q2[i], lam_k2=even_lam_k2[i], subln_g=even_subln_g[i],
                     w_o=bf(even_w_o[i]), ln1_g=even_ln1_g[i], ln1_b=even_ln1_b[i],
                     ffn_w1=bf(ffn_w1[i]), ffn_w3=bf(ffn_w3[i]), ffn_w2=bf(ffn_w2[i]),
                     ln2_g=even_ln2_g[i], ln2_b=even_ln2_b[i])
            xf = _even_layer(xf, p, layer, bsz, seq, tile)
        else:
            p = dict(w_in=bf(odd_w_in[i]), gn_g=odd_gn_g[i], w_o=bf(odd_w_o[i]),
                     ln1_g=odd_ln1_g[i], ln1_b=odd_ln1_b[i], router_w=router_w[i], router_b=router_b[i],
                     moe_w1=moe_w1_bf, moe_w3=moe_w3_bf, moe_w2=moe_w2_bf, moe_layer=i,
                     ln2_g=odd_ln2_g[i], ln2_b=odd_ln2_b[i])
            xf = _odd_layer(xf, p, bsz, seq, tile)
    return xf.reshape(bsz, seq, d)
```

```python
import functools
import math

import jax
import jax.numpy as jnp
from jax import lax
from jax.experimental import pallas as pl
from jax.experimental.pallas import tpu as pltpu

F32 = jnp.float32
BF16 = jnp.bfloat16
I32 = jnp.int32

D_MODEL = 1024
DEPTH = 4
DN_ALPHA = (2.0 * DEPTH) ** 0.25
LN_EPS = 1e-5

MIX_A = D_MODEL // 2
GM_GROUPS = 4
GM_DIM = MIX_A // GM_GROUPS
GM_CHUNK = 128
DA_HEADS = 4
DA_HEAD_DIM = 64
DA_V_DIM = 128
RET_HEADS = 4
RET_DK = D_MODEL // RET_HEADS
RET_DV = 2 * D_MODEL // RET_HEADS
ATTN_BLOCKS_PER_STEP = 2
RET_CHUNK = 256
D_FF = 7 * D_MODEL // 2
N_EXPERTS = 8
TOP_K = 2

V7X_VMEM_LIMIT = 56 * 1024 * 1024
SUBLANES = 8
NEG_BIG = -1e30
LOG2E = 1.4426950408889634

NT_DIMS = (((1,), (1,)), ((), ()))
TN_DIMS = (((0,), (0,)), ((), ()))


def _cparams(sem):
    return pltpu.CompilerParams(dimension_semantics=sem, vmem_limit_bytes=V7X_VMEM_LIMIT)


def _layer_norm(z, g, b):
    mu = jnp.mean(z, axis=-1, keepdims=True)
    zc = z - mu
    var = jnp.mean(zc * zc, axis=-1, keepdims=True)
    return zc * lax.rsqrt(var + LN_EPS) * g + b


def _gelu(x):
    return 0.5 * x * (1.0 + lax.erf(x * (2.0 ** -0.5)))


def _silu(x):
    return x * jax.nn.sigmoid(x)


def _matmul_kernel(x_ref, w_ref, o_ref, *, col_chunk):
    xb = x_ref[...].astype(BF16)
    for c0 in range(0, o_ref.shape[1], col_chunk):
        o_ref[:, c0:c0 + col_chunk] = jnp.dot(
            xb, w_ref[:, c0:c0 + col_chunk], preferred_element_type=F32).astype(o_ref.dtype)


def _matmul(x, w, *, tm, tn):
    m, k = x.shape
    n = w.shape[1]
    return pl.pallas_call(
        functools.partial(_matmul_kernel, col_chunk=512),
        grid=(n // tn, m // tm),
        in_specs=[pl.BlockSpec((tm, k), lambda j, i: (i, 0)),
                  pl.BlockSpec((k, tn), lambda j, i: (0, j))],
        out_specs=pl.BlockSpec((tm, tn), lambda j, i: (i, j)),
        out_shape=jax.ShapeDtypeStruct((m, n), BF16),
        compiler_params=_cparams(("parallel", "parallel")),
        name="in_proj",
    )(x, w)


def _proj_res_ln_kernel(*refs, n_lhs, row_chunk):
    a_refs = refs[:n_lhs]
    w_ref, r_ref, g_ref, b_ref, o_ref = refs[n_lhs:]
    for r0 in range(0, r_ref.shape[0], row_chunk):
        rows = slice(r0, r0 + row_chunk)
        y = None
        k0 = 0
        for a_ref in a_refs:
            kk = a_ref.shape[1]
            part = jnp.dot(a_ref[rows, :], w_ref[k0:k0 + kk, :], preferred_element_type=F32)
            y = part if y is None else y + part
            k0 += kk
        o_ref[rows, :] = _layer_norm(DN_ALPHA * r_ref[rows, :] + y, g_ref[...], b_ref[...])


def _proj_res_ln(a_list, w, res, g, b, *, tm):
    m, d = res.shape
    in_specs = [pl.BlockSpec((tm, a.shape[1]), lambda i: (i, 0)) for a in a_list]
    in_specs += [pl.BlockSpec(w.shape, lambda i: (0, 0), pipeline_mode=pl.Buffered(1)),
                 pl.BlockSpec((tm, d), lambda i: (i, 0)),
                 pl.BlockSpec((1, d), lambda i: (0, 0)),
                 pl.BlockSpec((1, d), lambda i: (0, 0))]
    return pl.pallas_call(
        functools.partial(_proj_res_ln_kernel, n_lhs=len(a_list), row_chunk=min(256, tm)),
        grid=(m // tm,),
        in_specs=in_specs,
        out_specs=pl.BlockSpec((tm, d), lambda i: (i, 0)),
        out_shape=jax.ShapeDtypeStruct((m, d), F32),
        compiler_params=_cparams(("parallel",)),
        name="out_proj_ln",
    )(*a_list, w, res, g.reshape(1, d), b.reshape(1, d))


def _in_proj_sgu_kernel(x_ref, w_ref, ws_ref, bst_ref, lg_ref, lb_ref, a_ref, h_ref, *, col_chunk, q_cols, q_scale):
    tm = x_ref.shape[0]
    xb = x_ref[...].astype(BF16)
    u_all = jnp.dot(xb, w_ref[:, 0:MIX_A], preferred_element_type=F32)
    v_all = jnp.dot(xb, w_ref[:, MIX_A:2 * MIX_A], preferred_element_type=F32)
    for c0 in range(2 * MIX_A, w_ref.shape[1], col_chunk):
        y = jnp.dot(xb, w_ref[:, c0:c0 + col_chunk], preferred_element_type=F32)
        if q_cols[0] <= c0 < q_cols[1]:
            y = y * q_scale
        h_ref[:, c0 - 2 * MIX_A:c0 - 2 * MIX_A + col_chunk] = y.astype(h_ref.dtype)

    row = lax.broadcasted_iota(I32, (GM_CHUNK, GM_CHUNK), 0)
    col = lax.broadcasted_iota(I32, (GM_CHUNK, GM_CHUNK), 1)
    causal = row >= col
    for g in range(GM_GROUPS):
        w = jnp.where(causal, ws_ref[g], 0.0).astype(BF16)
        bias = bst_ref[:, g:g + 1]
        cols = slice(g * GM_DIM, (g + 1) * GM_DIM)
        for r0 in range(0, tm, GM_CHUNK):
            rows = slice(r0, r0 + GM_CHUNK)
            vn = _layer_norm(_gelu(v_all[rows, cols]), lg_ref[g:g + 1, :], lb_ref[g:g + 1, :])
            mixed = jnp.dot(w, vn.astype(BF16), preferred_element_type=F32) + bias
            a_ref[rows, cols] = (_gelu(u_all[rows, cols]) * mixed).astype(a_ref.dtype)


def _in_proj_sgu(x, w, w_s, b_s, ln_g, ln_b, *, tm, q_scale):
    m, k = x.shape
    n = w.shape[1]
    col_chunk = 512
    q_cols = (2 * MIX_A, 2 * MIX_A + DA_HEADS * 2 * DA_HEAD_DIM)
    assert q_cols[0] % col_chunk == 0 and q_cols[1] % col_chunk == 0
    return pl.pallas_call(
        functools.partial(_in_proj_sgu_kernel, col_chunk=col_chunk, q_cols=q_cols, q_scale=q_scale),
        grid=(m // tm,),
        in_specs=[pl.BlockSpec((tm, k), lambda i: (i, 0)),
                  pl.BlockSpec((k, n), lambda i: (0, 0), pipeline_mode=pl.Buffered(1)),
                  pl.BlockSpec(w_s.shape, lambda i: (0, 0, 0)),
                  pl.BlockSpec((GM_CHUNK, GM_GROUPS), lambda i: (0, 0)),
                  pl.BlockSpec(ln_g.shape, lambda i: (0, 0)),
                  pl.BlockSpec(ln_b.shape, lambda i: (0, 0))],
        out_specs=[pl.BlockSpec((tm, MIX_A), lambda i: (i, 0)),
                   pl.BlockSpec((tm, n - 2 * MIX_A), lambda i: (i, 0))],
        out_shape=[jax.ShapeDtypeStruct((m, MIX_A), BF16),
                   jax.ShapeDtypeStruct((m, n - 2 * MIX_A), BF16)],
        compiler_params=_cparams(("parallel",)),
        name="in_proj_sgu",
    )(x, w, w_s, b_s.T, ln_g, ln_b)


def _alibi_features(slopes2, tq):
    rest = slopes2
    pieces = []
    for _ in range(3):
        piece = rest.astype(BF16).astype(F32)
        pieces.append(piece)
        rest = rest - piece
    pos = jnp.arange(tq, dtype=I32)
    hi = jnp.broadcast_to(((pos >> 4) << 4).astype(F32), (DA_HEADS, tq))
    lo = jnp.broadcast_to((pos & 15).astype(F32), (DA_HEADS, tq))
    qf = jnp.zeros((DA_HEADS, tq, DA_V_DIM), F32)
    kf = jnp.zeros((DA_HEADS, tq, DA_V_DIM), F32)
    for t, piece in enumerate(pieces):
        const = jnp.broadcast_to(piece[:, None], (DA_HEADS, tq))
        a, b = 2 * t, 2 * t + 1
        qf = qf.at[:, :, a].set(hi).at[:, :, b].set(lo)
        kf = kf.at[:, :, a].set(-const).at[:, :, b].set(-const)
        a, b = 6 + 2 * t, 7 + 2 * t
        qf = qf.at[:, :, a].set(const).at[:, :, b].set(const)
        kf = kf.at[:, :, a].set(hi).at[:, :, b].set(lo)
    return qf.astype(BF16), kf.astype(BF16)


def _diff_attn_kernel(sl_ref, lam_ref, q_ref, k_ref, v_ref, qf_ref, kf_ref, g_ref, o_ref,
                      sa_ref, sb_ref, acc_ref, m_ref, vt_ref, *, tq, lam_init):
    h = pl.program_id(1)
    step = pl.program_id(2)
    slope2 = sl_ref[h]
    lam = lam_ref[0]

    @pl.when(step == 0)
    def _():
        vt_ref[0:DA_V_DIM, :] = jnp.transpose(v_ref[0].astype(F32)).astype(BF16)
        sub = lax.broadcasted_iota(I32, (DA_V_DIM, vt_ref.shape[1]), 0)
        vt_ref[DA_V_DIM:2 * DA_V_DIM, :] = jnp.where(sub == 0, 1.0, 0.0).astype(BF16)

    lane = lax.broadcasted_iota(I32, (tq, DA_V_DIM), 1)
    keeps = (lane < DA_HEAD_DIM, lane >= DA_HEAD_DIM)
    kf = kf_ref[0]
    row = lax.broadcasted_iota(I32, (tq, tq), 0)
    col = lax.broadcasted_iota(I32, (tq, tq), 1)
    blocks_per_step = q_ref.shape[1] // tq

    def query_block(sub):
        qi = step * blocks_per_step + sub
        rows = slice(sub * tq, (sub + 1) * tq)
        q = q_ref[0, rows, :]
        zero = jnp.zeros_like(q)
        q_maps = tuple(jnp.concatenate([jnp.where(keep, q, zero), qf_ref[0]], axis=1) for keep in keeps)

        def scores(j, s_ref):
            start = pl.multiple_of(j * tq, tq)
            kj = jnp.concatenate([k_ref[0, pl.ds(start, tq), :], kf], axis=1)
            for c in range(2):
                s_ref[c] = lax.dot_general(kj, q_maps[c], NT_DIMS, preferred_element_type=F32)

        def update(j, s_ref, diagonal):
            start = pl.multiple_of(j * tq, tq)
            vt = vt_ref[:, pl.ds(start, tq)]
            off = -slope2 * ((qi - j) * tq).astype(F32)
            for c in range(2):
                s = s_ref[c]
                if diagonal:
                    s = jnp.where(col >= row, s, NEG_BIG)
                m = m_ref[c]
                m_new = jnp.maximum(m, jnp.max(s, axis=0, keepdims=True) + off)
                p = jnp.exp2(s - (m_new - off)).astype(BF16)
                acc_ref[c] = jnp.exp2(m - m_new) * acc_ref[c] + jnp.dot(vt, p, preferred_element_type=F32)
                m_ref[c] = m_new

        m_ref[...] = jnp.full(m_ref.shape, NEG_BIG, F32)
        acc_ref[...] = jnp.zeros_like(acc_ref)

        scores(0, sa_ref)

        def pair(t, carry):
            b = 2 * t
            scores(b + 1, sb_ref)
            update(b, sa_ref, False)
            scores(b + 2, sa_ref)
            update(b + 1, sb_ref, False)
            return carry

        lax.fori_loop(0, qi // 2, pair, 0)

        @pl.when(qi % 2 == 1)
        def _():
            scores(qi, sb_ref)
            update(qi - 1, sa_ref, False)
            update(qi, sb_ref, True)

        @pl.when(qi % 2 == 0)
        def _():
            update(qi, sa_ref, True)

        acc_a = acc_ref[0]
        acc_b = acc_ref[1]
        o = (acc_a[:DA_V_DIM, :] / acc_a[DA_V_DIM:DA_V_DIM + 1, :]
             - lam * (acc_b[:DA_V_DIM, :] / acc_b[DA_V_DIM:DA_V_DIM + 1, :]))
        o = o * lax.rsqrt(jnp.mean(o * o, axis=0, keepdims=True) + LN_EPS) * g_ref[...]
        o_ref[0, rows, :] = (jnp.transpose(o) * (1.0 - lam_init)).astype(o_ref.dtype)

    for sub in range(blocks_per_step):
        query_block(sub)


def _diff_attn(h3, slopes, lam, subln_g, *, tq, lam_init):
    bsz, s, _ = h3.shape
    slopes2 = slopes * LOG2E
    qf, kf = _alibi_features(slopes2, tq)
    blk = DA_V_DIM
    q_blk0 = 0
    k_blk0 = q_blk0 + DA_HEADS
    v_blk0 = k_blk0 + DA_HEADS
    smem = pl.BlockSpec(memory_space=pltpu.SMEM)
    tstep = min(ATTN_BLOCKS_PER_STEP * tq, s)
    return pl.pallas_call(
        functools.partial(_diff_attn_kernel, tq=tq, lam_init=lam_init),
        grid=(bsz, DA_HEADS, s // tstep),
        in_specs=[smem, smem,
                  pl.BlockSpec((1, tstep, blk), lambda b, h, i: (b, i, q_blk0 + h)),
                  pl.BlockSpec((1, s, blk), lambda b, h, i: (b, 0, k_blk0 + h)),
                  pl.BlockSpec((1, s, blk), lambda b, h, i: (b, 0, v_blk0 + h)),
                  pl.BlockSpec((1, tq, blk), lambda b, h, i: (h, 0, 0)),
                  pl.BlockSpec((1, tq, blk), lambda b, h, i: (h, 0, 0)),
                  pl.BlockSpec((blk, 1), lambda b, h, i: (0, 0))],
        out_specs=pl.BlockSpec((1, tstep, blk), lambda b, h, i: (b, i, h)),
        out_shape=jax.ShapeDtypeStruct((bsz, s, DA_HEADS * DA_V_DIM), BF16),
        scratch_shapes=[pltpu.VMEM((2, tq, tq), F32), pltpu.VMEM((2, tq, tq), F32),
                        pltpu.VMEM((2, 2 * DA_V_DIM, tq), F32), pltpu.VMEM((2, 1, tq), F32),
                        pltpu.VMEM((2 * DA_V_DIM, s), BF16)],
        compiler_params=_cparams(("parallel", "parallel", "arbitrary")),
        name="diff_attn",
    )(slopes2, lam, h3, h3, h3, qf, kf, subln_g.reshape(blk, 1))


def _retention_kernel(q_ref, k_ref, v_ref, g_ref, dec_ref, qd_ref, kd_ref, gn_ref, o_ref, state_ref):
    tr = q_ref.shape[1]

    @pl.when(pl.program_id(1) == 0)
    def _():
        state_ref[...] = jnp.zeros_like(state_ref)

    for h in range(RET_HEADS):
        inner_decay = dec_ref[h]
        q_decay = qd_ref[h]
        k_decay = kd_ref[h]
        chunk_decay = math.exp(math.log1p(-(2.0 ** (-5.0 - h))) * RET_CHUNK)
        kc = slice(h * RET_DK, (h + 1) * RET_DK)
        vc = slice(h * RET_DV, (h + 1) * RET_DV)
        for c0 in range(0, tr, RET_CHUNK):
            rows = slice(c0, c0 + RET_CHUNK)
            q = q_ref[0, rows, kc]
            k = k_ref[0, rows, kc] * (RET_DK ** -0.5)
            v = v_ref[0, rows, vc]
            state = state_ref[h]
            scores = lax.dot_general(q, k, NT_DIMS, preferred_element_type=F32) * inner_decay
            inner = jnp.dot(scores.astype(BF16), v, preferred_element_type=F32)
            cross = jnp.dot(q, state.astype(BF16), preferred_element_type=F32) * q_decay
            k_dec = (k.astype(F32) * k_decay).astype(BF16)
            state_ref[h] = state * chunk_decay + lax.dot_general(
                k_dec, v, TN_DIMS, preferred_element_type=F32)
            o = inner + cross
            mu = jnp.mean(o, axis=-1, keepdims=True)
            oc = o - mu
            var = jnp.mean(oc * oc, axis=-1, keepdims=True)
            on = oc * lax.rsqrt(var + LN_EPS) * gn_ref[h:h + 1, :]
            gate = g_ref[0, rows, vc].astype(F32)
            o_ref[0, rows, vc] = (_silu(gate) * on).astype(o_ref.dtype)


def _retention(h3, gn_g, *, tr):
    bsz, s, _ = h3.shape
    dq = RET_HEADS * RET_DK
    dv = RET_HEADS * RET_DV
    log_g = jnp.log1p(-(2.0 ** (-5.0 - jnp.arange(RET_HEADS, dtype=F32))))[:, None, None]
    idx = jnp.arange(RET_CHUNK, dtype=F32)
    rel = idx[:, None] - idx[None, :]
    inner_decay = jnp.where(rel >= 0, jnp.exp(log_g * jnp.maximum(rel, 0.0)), 0.0)
    q_decay = jnp.exp(log_g * (idx[None, :, None] + 1.0))
    k_decay = jnp.exp(log_g * (RET_CHUNK - 1.0 - idx[None, :, None]))
    whole = lambda a: pl.BlockSpec(a.shape, lambda b, i: (0,) * a.ndim)
    return pl.pallas_call(
        _retention_kernel,
        grid=(bsz, s // tr),
        in_specs=[pl.BlockSpec((1, tr, dq), lambda b, i: (b, i, 0)),
                  pl.BlockSpec((1, tr, dq), lambda b, i: (b, i, 1)),
                  pl.BlockSpec((1, tr, dv), lambda b, i: (b, i, 1)),
                  pl.BlockSpec((1, tr, dv), lambda b, i: (b, i, 2)),
                  whole(inner_decay), whole(q_decay), whole(k_decay),
                  pl.BlockSpec(gn_g.shape, lambda b, i: (0, 0))],
        out_specs=pl.BlockSpec((1, tr, dv), lambda b, i: (b, i, 0)),
        out_shape=jax.ShapeDtypeStruct((bsz, s, dv), BF16),
        scratch_shapes=[pltpu.VMEM((RET_HEADS, RET_DK, RET_DV), F32)],
        compiler_params=_cparams(("parallel", "arbitrary")),
        name="retention",
    )(h3, h3, h3, h3, inner_decay, q_decay, k_decay, gn_g)


def _swiglu(xb, w1, w3, w2, tf):
    acc = None
    for f0 in range(0, w2.shape[0], tf):
        h1 = jnp.dot(xb, w1[:, f0:f0 + tf], preferred_element_type=F32)
        h3 = jnp.dot(xb, w3[:, f0:f0 + tf], preferred_element_type=F32)
        part = jnp.dot((_silu(h1) * h3).astype(BF16), w2[f0:f0 + tf, :], preferred_element_type=F32)
        acc = part if acc is None else acc + part
    return acc


def _ffn_kernel(x_ref, w1_ref, w3_ref, w2_ref, g_ref, b_ref, o_ref, *, tf, row_chunk):
    for r0 in range(0, x_ref.shape[0], row_chunk):
        rows = slice(r0, r0 + row_chunk)
        x = x_ref[rows, :]
        y = _swiglu(x.astype(BF16), w1_ref, w3_ref, w2_ref, tf)
        o_ref[rows, :] = _layer_norm(DN_ALPHA * x + y, g_ref[...], b_ref[...])


def _ffn(x, w1, w3, w2, g, b, *, tm, tf):
    m, d = x.shape
    resident = dict(pipeline_mode=pl.Buffered(1))
    return pl.pallas_call(
        functools.partial(_ffn_kernel, tf=tf, row_chunk=min(512, tm)),
        grid=(m // tm,),
        in_specs=[pl.BlockSpec((tm, d), lambda i: (i, 0)),
                  pl.BlockSpec(w1.shape, lambda i: (0, 0), **resident),
                  pl.BlockSpec(w3.shape, lambda i: (0, 0), **resident),
                  pl.BlockSpec(w2.shape, lambda i: (0, 0), **resident),
                  pl.BlockSpec((1, d), lambda i: (0, 0)),
                  pl.BlockSpec((1, d), lambda i: (0, 0))],
        out_specs=pl.BlockSpec((tm, d), lambda i: (i, 0)),
        out_shape=jax.ShapeDtypeStruct((m, d), F32),
        compiler_params=_cparams(("parallel",)),
        name="ffn",
    )(x, w1, w3, w2, g.reshape(1, d), b.reshape(1, d))


def _experts_kernel(te_ref, nv_ref, xs_ref, w1_ref, w3_ref, w2_ref, ys_ref, *, tf):
    del te_ref
    i = pl.program_id(0)

    @pl.when(i < nv_ref[0])
    def _():
        ys_ref[...] = _swiglu(xs_ref[...].astype(BF16), w1_ref.at[0, 0], w3_ref.at[0, 0], w2_ref.at[0, 0], tf)

    @pl.when(i >= nv_ref[0])
    def _():
        ys_ref[...] = jnp.zeros_like(ys_ref)


def _experts(xs, tile_expert, n_valid, w1, w3, w2, layer, *, tm, tf):
    r, d = xs.shape
    resident = dict(pipeline_mode=pl.Buffered(1))

    def expert_block(w):
        return pl.BlockSpec((1, 1) + w.shape[2:], lambda i, te, nv: (layer, te[i], 0, 0), **resident)

    grid_spec = pltpu.PrefetchScalarGridSpec(
        num_scalar_prefetch=2,
        grid=(r // tm,),
        in_specs=[pl.BlockSpec((tm, d), lambda i, te, nv: (jnp.minimum(i, nv[0] - 1), 0)),
                  expert_block(w1), expert_block(w3), expert_block(w2)],
        out_specs=pl.BlockSpec((tm, d), lambda i, te, nv: (i, 0)),
    )
    return pl.pallas_call(
        functools.partial(_experts_kernel, tf=tf),
        grid_spec=grid_spec,
        out_shape=jax.ShapeDtypeStruct((r, d), F32),
        compiler_params=_cparams(("arbitrary",)),
        name="experts",
    )(tile_expert, n_valid, xs, w1, w3, w2)


def _split_bf16(x):
    hi = x.astype(BF16)
    return hi, (x - hi.astype(F32)).astype(BF16)


def _router_kernel(x_ref, wt_ref, b_ref, meta_ref, gate_ref, cnt_ref, carry_ref, tri_ref):
    i = pl.program_id(0)
    tm = x_ref.shape[0]

    @pl.when(i == 0)
    def _():
        carry_ref[...] = jnp.zeros_like(carry_ref)
        a = lax.broadcasted_iota(I32, (tm, tm), 0)
        b = lax.broadcasted_iota(I32, (tm, tm), 1)
        tri_ref[...] = (a <= b).astype(BF16)

    xh, xl = _split_bf16(x_ref[...])
    wh, wl = _split_bf16(wt_ref[...])
    logits = (lax.dot_general(wh, xh, NT_DIMS, preferred_element_type=F32)
              + lax.dot_general(wh, xl, NT_DIMS, preferred_element_type=F32)
              + lax.dot_general(wl, xh, NT_DIMS, preferred_element_type=F32)
              + b_ref[...])

    e_iota = lax.broadcasted_iota(I32, logits.shape, 0)
    m1 = jnp.max(logits, axis=0, keepdims=True)
    i1 = jnp.min(jnp.where(logits == m1, e_iota, N_EXPERTS), axis=0, keepdims=True)
    rest = jnp.where(e_iota == i1, -jnp.inf, logits)
    m2 = jnp.max(rest, axis=0, keepdims=True)
    i2 = jnp.min(jnp.where(rest == m2, e_iota, N_EXPERTS), axis=0, keepdims=True)
    e2 = jnp.exp(m2 - m1)
    gate_ref[0:1, :] = 1.0 / (1.0 + e2)
    gate_ref[1:2, :] = e2 / (1.0 + e2)

    pick1 = e_iota == i1
    pick2 = e_iota == i2
    chosen = jnp.where(pick1 | pick2, 1.0, 0.0)
    incl = jnp.dot(chosen.astype(BF16), tri_ref[...], preferred_element_type=F32)
    before = carry_ref[...] + incl - chosen
    meta_ref[0:1, :] = i1
    meta_ref[1:2, :] = i2
    meta_ref[2:3, :] = jnp.sum(jnp.where(pick1, before, 0.0), axis=0, keepdims=True).astype(I32)
    meta_ref[3:4, :] = jnp.sum(jnp.where(pick2, before, 0.0), axis=0, keepdims=True).astype(I32)
    carry_ref[...] += incl[:, tm - 1:tm]
    cnt_ref[...] = jnp.broadcast_to(carry_ref[...], cnt_ref.shape).astype(I32)


def _router(x, w_router, b_router, *, tm):
    m, d = x.shape
    return pl.pallas_call(
        _router_kernel,
        grid=(m // tm,),
        in_specs=[pl.BlockSpec((tm, d), lambda i: (i, 0)),
                  pl.BlockSpec((N_EXPERTS, d), lambda i: (0, 0)),
                  pl.BlockSpec((N_EXPERTS, 1), lambda i: (0, 0))],
        out_specs=[pl.BlockSpec((4, tm), lambda i: (0, i)),
                   pl.BlockSpec((TOP_K, tm), lambda i: (0, i)),
                   pl.BlockSpec((N_EXPERTS, 128), lambda i: (0, 0))],
        out_shape=[jax.ShapeDtypeStruct((4, m), I32),
                   jax.ShapeDtypeStruct((TOP_K, m), F32),
                   jax.ShapeDtypeStruct((N_EXPERTS, 128), I32)],
        scratch_shapes=[pltpu.VMEM((N_EXPERTS, 1), F32), pltpu.VMEM((tm, tm), BF16)],
        compiler_params=_cparams(("arbitrary",)),
        name="router",
    )(x, w_router.T, b_router.reshape(N_EXPERTS, 1))


ROW_DMA_UNROLL = 8


def _for_each_row(tt, fn):
    def trip(t, carry):
        for k in range(ROW_DMA_UNROLL):
            fn(t * ROW_DMA_UNROLL + k, k)
        return carry

    lax.fori_loop(0, tt // ROW_DMA_UNROLL, trip, 0)


def _dispatch_kernel(dest_ref, dest_prev_ref, pad_ref, x_ref, xs_ref, stage_ref, zero_ref, sem, zero_sem, *, tt):
    i = pl.program_id(0)
    slot = i % 2

    def zero_copy(e):
        start = pl.multiple_of(pad_ref[e], SUBLANES)
        return pltpu.make_async_copy(zero_ref, xs_ref.at[pl.ds(start, zero_ref.shape[0]), :], zero_sem)

    te = zero_ref.shape[0] - SUBLANES
    n_tiles = xs_ref.shape[0] // te

    def tail_copy(t):
        return pltpu.make_async_copy(zero_ref.at[pl.ds(0, te), :],
                                     xs_ref.at[pl.ds(pl.multiple_of(t * te, te), te), :], zero_sem)

    @pl.when(i == 0)
    def _():
        zero_ref[...] = jnp.zeros_like(zero_ref)
        for e in range(N_EXPERTS):
            zero_copy(e).start()
            zero_copy(e).wait()
        for e in range(N_EXPERTS):
            t = pad_ref[N_EXPERTS] + e

            @pl.when(t < n_tiles)
            def _():
                tail_copy(t).start()
                tail_copy(t).wait()

    def copy(idx_ref, r, j, s):
        return pltpu.make_async_copy(stage_ref.at[s, pl.ds(r, 1), :],
                                     xs_ref.at[pl.ds(idx_ref[0, 0, j * tt + r], 1), :], sem.at[s])

    def wait_all(idx_ref, s):
        _for_each_row(tt, lambda r, k: [copy(idx_ref, r, j, s).wait() for j in range(TOP_K)])

    stage_ref[slot] = x_ref[...]
    _for_each_row(tt, lambda r, k: [copy(dest_ref, r, j, slot).start() for j in range(TOP_K)])

    @pl.when(i > 0)
    def _():
        wait_all(dest_prev_ref, 1 - slot)

    @pl.when(i == pl.num_programs(0) - 1)
    def _():
        wait_all(dest_ref, slot)


def _dispatch(x, dest_tiles, pad_start, rows, *, tt, te):
    m, d = x.shape
    idx_block = (1, 1, TOP_K * tt)
    return pl.pallas_call(
        functools.partial(_dispatch_kernel, tt=tt),
        grid=(m // tt,),
        in_specs=[pl.BlockSpec(idx_block, lambda i: (i, 0, 0), memory_space=pltpu.SMEM),
                  pl.BlockSpec(idx_block, lambda i: (jnp.maximum(i - 1, 0), 0, 0), memory_space=pltpu.SMEM),
                  pl.BlockSpec(memory_space=pltpu.SMEM),
                  pl.BlockSpec((tt, d), lambda i: (i, 0))],
        out_specs=pl.BlockSpec(memory_space=pl.ANY),
        out_shape=jax.ShapeDtypeStruct((rows, d), F32),
        scratch_shapes=[pltpu.VMEM((2, tt, d), F32), pltpu.VMEM((te + SUBLANES, d), F32),
                        pltpu.SemaphoreType.DMA((2,)), pltpu.SemaphoreType.DMA(())],
        compiler_params=_cparams(("arbitrary",)),
        name="dispatch",
    )(dest_tiles, dest_tiles, pad_start, x)


def _combine_kernel(dest_ref, dest_next_ref, x_ref, gate_ref, g_ref, b_ref, ys_ref, o_ref, buf_ref, sem, *, tt):
    i = pl.program_id(0)
    slot = i % 2

    def copy(idx_ref, r, j, s):
        return pltpu.make_async_copy(ys_ref.at[pl.ds(idx_ref[0, 0, j * tt + r], 1), :],
                                     buf_ref.at[s, j, pl.ds(r, 1), :], sem.at[s])

    def start_all(idx_ref, s):
        _for_each_row(tt, lambda r, k: [copy(idx_ref, r, j, s).start(priority=(k + j) % 2) for j in range(TOP_K)])

    @pl.when(i == 0)
    def _():
        start_all(dest_ref, slot)

    @pl.when(i + 1 < pl.num_programs(0))
    def _():
        start_all(dest_next_ref, 1 - slot)

    _for_each_row(tt, lambda r, k: [copy(dest_ref, r, j, slot).wait() for j in range(TOP_K)])

    y = gate_ref[:, 0:1] * buf_ref[slot, 0] + gate_ref[:, 1:2] * buf_ref[slot, 1]
    o_ref[...] = _layer_norm(DN_ALPHA * x_ref[...] + y, g_ref[...], b_ref[...])


def _combine(x, ys, dest_tiles, gates_t, g, b, *, tt):
    m, d = x.shape
    n = m // tt
    idx_block = (1, 1, TOP_K * tt)
    return pl.pallas_call(
        functools.partial(_combine_kernel, tt=tt),
        grid=(n,),
        in_specs=[pl.BlockSpec(idx_block, lambda i: (i, 0, 0), memory_space=pltpu.SMEM),
                  pl.BlockSpec(idx_block, lambda i: (jnp.minimum(i + 1, n - 1), 0, 0), memory_space=pltpu.SMEM),
                  pl.BlockSpec((tt, d), lambda i: (i, 0)),
                  pl.BlockSpec((tt, TOP_K), lambda i: (i, 0)),
                  pl.BlockSpec((1, d), lambda i: (0, 0)),
                  pl.BlockSpec((1, d), lambda i: (0, 0)),
                  pl.BlockSpec(memory_space=pl.ANY)],
        out_specs=pl.BlockSpec((tt, d), lambda i: (i, 0)),
        out_shape=jax.ShapeDtypeStruct((m, d), F32),
        scratch_shapes=[pltpu.VMEM((2, TOP_K, tt, d), F32), pltpu.SemaphoreType.DMA((2,))],
        compiler_params=_cparams(("arbitrary",)),
        name="combine",
    )(dest_tiles, dest_tiles, x, gates_t, g.reshape(1, d), b.reshape(1, d), ys)


def _moe(x, w_router, b_router, w1, w3, w2, layer, g, b, *, tile):
    m, d = x.shape
    meta, gates, counts = _router(x, w_router, b_router, tm=tile.router)
    counts = counts[:, 0]
    te = tile.expert
    padded = (counts + te - 1) // te * te
    ends = jnp.cumsum(padded)
    starts = ends - padded
    expert_ids = meta[0:TOP_K]
    first_row = sum(jnp.where(expert_ids == e, starts[e], 0) for e in range(N_EXPERTS))
    dest = first_row + meta[TOP_K:2 * TOP_K]
    n_tiles = TOP_K * m // te + N_EXPERTS
    tile_start = jnp.arange(n_tiles, dtype=I32) * te
    tile_expert = jnp.minimum(jnp.sum((tile_start[:, None] >= ends[None, :]).astype(I32), axis=1), N_EXPERTS - 1)
    n_valid = (ends[-1:] // te).astype(I32)

    rows = n_tiles * te
    pad_start = jnp.minimum((starts + counts) // SUBLANES * SUBLANES, rows - te - SUBLANES).astype(I32)
    pad_start = jnp.concatenate([pad_start, n_valid])

    tt = tile.token
    dest_tiles = dest.reshape(TOP_K, m // tt, tt).transpose(1, 0, 2).reshape(m // tt, 1, TOP_K * tt)
    xs = _dispatch(x, dest_tiles, pad_start, rows, tt=tt, te=te)
    ys = _experts(xs, tile_expert, n_valid, w1, w3, w2, layer, tm=te, tf=tile.ff)
    return _combine(x, ys, dest_tiles, gates.T, g, b, tt=tt)


class _Tiles:
    def __init__(self, bsz, seq):
        m = bsz * seq
        self.proj = min(1024, m)
        self.attn = min(512, seq)
        self.ret = min(512, seq)
        self.ffn = min(1024, m)
        self.ff = 512
        self.router = min(1024, m)
        self.expert = min(1024, m)
        self.token = min(512, m)


def _even_layer(x, p, layer_idx, bsz, seq, tile):
    m, d = x.shape
    a_out, h = _in_proj_sgu(x, p["w_in"], p["w_s"], p["b_s"], p["v_ln_g"], p["v_ln_b"],
                            tm=tile.proj, q_scale=DA_HEAD_DIM ** -0.5 * LOG2E)
    lam_init = 0.8 - 0.6 * math.exp(-0.3 * layer_idx)
    lam = (jnp.exp(jnp.sum(p["lam_q1"] * p["lam_k1"])) - jnp.exp(jnp.sum(p["lam_q2"] * p["lam_k2"]))
           + lam_init).reshape(1).astype(F32)
    slopes = 2.0 ** (-8.0 * jnp.arange(1, DA_HEADS + 1, dtype=F32) / DA_HEADS)
    b_out = _diff_attn(h.reshape(bsz, seq, -1), slopes, lam, p["subln_g"], tq=tile.attn, lam_init=lam_init)
    x = _proj_res_ln([a_out, b_out.reshape(m, -1)], p["w_o"], x, p["ln1_g"], p["ln1_b"], tm=tile.proj)
    return _ffn(x, p["ffn_w1"], p["ffn_w3"], p["ffn_w2"], p["ln2_g"], p["ln2_b"], tm=tile.ffn, tf=tile.ff)


def _odd_layer(x, p, bsz, seq, tile):
    m, d = x.shape
    n_in = p["w_in"].shape[1]
    h = _matmul(x, p["w_in"], tm=tile.proj, tn=n_in // 2)
    o = _retention(h.reshape(bsz, seq, n_in), p["gn_g"], tr=tile.ret)
    x = _proj_res_ln([o.reshape(m, -1)], p["w_o"], x, p["ln1_g"], p["ln1_b"], tm=tile.proj)
    return _moe(x, p["router_w"], p["router_b"], p["moe_w1"], p["moe_w3"], p["moe_w2"], p["moe_layer"],
                p["ln2_g"], p["ln2_b"], tile=tile)


def kernel(x, even_w_in, even_w_s, even_b_s, even_v_ln_g, even_v_ln_b, even_lam_q1, even_lam_k1, even_lam_q2, even_lam_k2, even_subln_g, even_w_o, even_ln1_g, even_ln1_b, ffn_w1, ffn_w3, ffn_w2, even_ln2_g, even_ln2_b, odd_w_in, odd_gn_g, odd_w_o, odd_ln1_g, odd_ln1_b, router_w, router_b, moe_w1, moe_w3, moe_w2, odd_ln2_g, odd_ln2_b):
    bsz, seq, d = x.shape
    tile = _Tiles(bsz, seq)
    bf = lambda w: w.astype(BF16)
    moe_w1_bf, moe_w3_bf, moe_w2_bf = bf(moe_w1), bf(moe_w3), bf(moe_w2)
    xf = x.reshape(bsz * seq, d)
    for layer in range(DEPTH):
        i = layer // 2
        if layer % 2 == 0:
            p = dict(w_in=bf(even_w_in[i]), w_s=even_w_s[i], b_s=even_b_s[i], v_ln_g=even_v_ln_g[i],
                     v_ln_b=even_v_ln_b[i], lam_q1=even_lam_q1[i], lam_k1=even_lam_k1[i],
                     lam_q2=even_lam_q2[i], lam_k2=even_lam_k2[i], subln_g=even_subln_g[i],
                     w_o=bf(even_w_o[i]), ln1_g=even_ln1_g[i], ln1_b=even_ln1_b[i],
                     ffn_w1=bf(ffn_w1[i]), ffn_w3=bf(ffn_w3[i]), ffn_w2=bf(ffn_w2[i]),
                     ln2_g=even_ln2_g[i], ln2_b=even_ln2_b[i])
            xf = _even_layer(xf, p, layer, bsz, seq, tile)
        else:
            p = dict(w_in=bf(odd_w_in[i]), gn_g=odd_gn_g[i], w_o=bf(odd_w_o[i]),
                     ln1_g=odd_ln1_g[i], ln1_b=odd_ln1_b[i], router_w=router_w[i], router_b=router_b[i],
                     moe_w1=moe_w1_bf, moe_w3=moe_w3_bf, moe_w2=moe_w2_bf, moe_layer=i,
                     ln2_g=odd_ln2_g[i], ln2_b=odd_ln2_b[i])
            xf = _odd_layer(xf, p, bsz, seq, tile)
    return xf.reshape(bsz, seq, d)
```

```python
import functools
import math

import jax
import jax.numpy as jnp
from jax import lax
from jax.experimental import pallas as pl
from jax.experimental.pallas import tpu as pltpu

F32 = jnp.float32
BF16 = jnp.bfloat16
I32 = jnp.int32

D_MODEL = 1024
DEPTH = 4
DN_ALPHA = (2.0 * DEPTH) ** 0.25
LN_EPS = 1e-5

MIX_A = D_MODEL // 2
GM_GROUPS = 4
GM_DIM = MIX_A // GM_GROUPS
GM_CHUNK = 128
DA_HEADS = 4
DA_HEAD_DIM = 64
DA_V_DIM = 128
RET_HEADS = 4
RET_DK = D_MODEL // RET_HEADS
RET_DV = 2 * D_MODEL // RET_HEADS
ATTN_BLOCKS_PER_STEP = 2
RET_CHUNK = 256
D_FF = 7 * D_MODEL // 2
N_EXPERTS = 8
TOP_K = 2

V7X_VMEM_LIMIT = 56 * 1024 * 1024
SUBLANES = 8
NEG_BIG = -1e30
LOG2E = 1.4426950408889634

NT_DIMS = (((1,), (1,)), ((), ()))
TN_DIMS = (((0,), (0,)), ((), ()))


def _cparams(sem):
    return pltpu.CompilerParams(dimension_semantics=sem, vmem_limit_bytes=V7X_VMEM_LIMIT)


def _layer_norm(z, g, b):
    mu = jnp.mean(z, axis=-1, keepdims=True)
    zc = z - mu
    var = jnp.mean(zc * zc, axis=-1, keepdims=True)
    return zc * lax.rsqrt(var + LN_EPS) * g + b


def _gelu(x):
    return 0.5 * x * (1.0 + lax.erf(x * (2.0 ** -0.5)))


def _silu(x):
    return x * jax.nn.sigmoid(x)


def _matmul_kernel(x_ref, w_ref, o_ref, *, col_chunk):
    xb = x_ref[...].astype(BF16)
    for c0 in range(0, o_ref.shape[1], col_chunk):
        o_ref[:, c0:c0 + col_chunk] = jnp.dot(
            xb, w_ref[:, c0:c0 + col_chunk], preferred_element_type=F32).astype(o_ref.dtype)


def _matmul(x, w, *, tm, tn):
    m, k = x.shape
    n = w.shape[1]
    return pl.pallas_call(
        functools.partial(_matmul_kernel, col_chunk=512),
        grid=(n // tn, m // tm),
        in_specs=[pl.BlockSpec((tm, k), lambda j, i: (i, 0)),
                  pl.BlockSpec((k, tn), lambda j, i: (0, j))],
        out_specs=pl.BlockSpec((tm, tn), lambda j, i: (i, j)),
        out_shape=jax.ShapeDtypeStruct((m, n), BF16),
        compiler_params=_cparams(("parallel", "parallel")),
        name="in_proj",
    )(x, w)


def _proj_res_ln_kernel(*refs, n_lhs, row_chunk):
    a_refs = refs[:n_lhs]
    w_ref, r_ref, g_ref, b_ref, o_ref = refs[n_lhs:]
    for r0 in range(0, r_ref.shape[0], row_chunk):
        rows = slice(r0, r0 + row_chunk)
        y = None
        k0 = 0
        for a_ref in a_refs:
            kk = a_ref.shape[1]
            part = jnp.dot(a_ref[rows, :], w_ref[k0:k0 + kk, :], preferred_element_type=F32)
            y = part if y is None else y + part
            k0 += kk
        o_ref[rows, :] = _layer_norm(DN_ALPHA * r_ref[rows, :] + y, g_ref[...], b_ref[...])


def _proj_res_ln(a_list, w, res, g, b, *, tm):
    m, d = res.shape
    in_specs = [pl.BlockSpec((tm, a.shape[1]), lambda i: (i, 0)) for a in a_list]
    in_specs += [pl.BlockSpec(w.shape, lambda i: (0, 0), pipeline_mode=pl.Buffered(1)),
                 pl.BlockSpec((tm, d), lambda i: (i, 0)),
                 pl.BlockSpec((1, d), lambda i: (0, 0)),
                 pl.BlockSpec((1, d), lambda i: (0, 0))]
    return pl.pallas_call(
        functools.partial(_proj_res_ln_kernel, n_lhs=len(a_list), row_chunk=min(256, tm)),
        grid=(m // tm,),
        in_specs=in_specs,
        out_specs=pl.BlockSpec((tm, d), lambda i: (i, 0)),
        out_shape=jax.ShapeDtypeStruct((m, d), F32),
        compiler_params=_cparams(("parallel",)),
        name="out_proj_ln",
    )(*a_list, w, res, g.reshape(1, d), b.reshape(1, d))


def _in_proj_sgu_kernel(x_ref, w_ref, ws_ref, bst_ref, lg_ref, lb_ref, a_ref, h_ref, *, col_chunk, q_cols, q_scale):
    tm = x_ref.shape[0]
    xb = x_ref[...].astype(BF16)
    u_all = jnp.dot(xb, w_ref[:, 0:MIX_A], preferred_element_type=F32)
    v_all = jnp.dot(xb, w_ref[:, MIX_A:2 * MIX_A], preferred_element_type=F32)
    for c0 in range(2 * MIX_A, w_ref.shape[1], col_chunk):
        y = jnp.dot(xb, w_ref[:, c0:c0 + col_chunk], preferred_element_type=F32)
        if q_cols[0] <= c0 < q_cols[1]:
            y = y * q_scale
        h_ref[:, c0 - 2 * MIX_A:c0 - 2 * MIX_A + col_chunk] = y.astype(h_ref.dtype)

    row = lax.broadcasted_iota(I32, (GM_CHUNK, GM_CHUNK), 0)
    col = lax.broadcasted_iota(I32, (GM_CHUNK, GM_CHUNK), 1)
    causal = row >= col
    for g in range(GM_GROUPS):
        w = jnp.where(causal, ws_ref[g], 0.0).astype(BF16)
        bias = bst_ref[:, g:g + 1]
        cols = slice(g * GM_DIM, (g + 1) * GM_DIM)
        for r0 in range(0, tm, GM_CHUNK):
            rows = slice(r0, r0 + GM_CHUNK)
            vn = _layer_norm(_gelu(v_all[rows, cols]), lg_ref[g:g + 1, :], lb_ref[g:g + 1, :])
            mixed = jnp.dot(w, vn.astype(BF16), preferred_element_type=F32) + bias
            a_ref[rows, cols] = (_gelu(u_all[rows, cols]) * mixed).astype(a_ref.dtype)


def _in_proj_sgu(x, w, w_s, b_s, ln_g, ln_b, *, tm, q_scale):
    m, k = x.shape
    n = w.shape[1]
    col_chunk = 512
    q_cols = (2 * MIX_A, 2 * MIX_A + DA_HEADS * 2 * DA_HEAD_DIM)
    assert q_cols[0] % col_chunk == 0 and q_cols[1] % col_chunk == 0
    return pl.pallas_call(
        functools.partial(_in_proj_sgu_kernel, col_chunk=col_chunk, q_cols=q_cols, q_scale=q_scale),
        grid=(m // tm,),
        in_specs=[pl.BlockSpec((tm, k), lambda i: (i, 0)),
                  pl.BlockSpec((k, n), lambda i: (0, 0), pipeline_mode=pl.Buffered(1)),
                  pl.BlockSpec(w_s.shape, lambda i: (0, 0, 0)),
                  pl.BlockSpec((GM_CHUNK, GM_GROUPS), lambda i: (0, 0)),
                  pl.BlockSpec(ln_g.shape, lambda i: (0, 0)),
                  pl.BlockSpec(ln_b.shape, lambda i: (0, 0))],
        out_specs=[pl.BlockSpec((tm, MIX_A), lambda i: (i, 0)),
                   pl.BlockSpec((tm, n - 2 * MIX_A), lambda i: (i, 0))],
        out_shape=[jax.ShapeDtypeStruct((m, MIX_A), BF16),
                   jax.ShapeDtypeStruct((m, n - 2 * MIX_A), BF16)],
        compiler_params=_cparams(("parallel",)),
        name="in_proj_sgu",
    )(x, w, w_s, b_s.T, ln_g, ln_b)


def _alibi_features(slopes2, tq):
    rest = slopes2
    pieces = []
    for _ in range(3):
        piece = rest.astype(BF16).astype(F32)
        pieces.append(piece)
        rest = rest - piece
    pos = jnp.arange(tq, dtype=I32)
    hi = jnp.broadcast_to(((pos >> 4) << 4).astype(F32), (DA_HEADS, tq))
    lo = jnp.broadcast_to((pos & 15).astype(F32), (DA_HEADS, tq))
    qf = jnp.zeros((DA_HEADS, tq, DA_V_DIM), F32)
    kf = jnp.zeros((DA_HEADS, tq, DA_V_DIM), F32)
    for t, piece in enumerate(pieces):
        const = jnp.broadcast_to(piece[:, None], (DA_HEADS, tq))
        a, b = 2 * t, 2 * t + 1
        qf = qf.at[:, :, a].set(hi).at[:, :, b].set(lo)
        kf = kf.at[:, :, a].set(-const).at[:, :, b].set(-const)
        a, b = 6 + 2 * t, 7 + 2 * t
        qf = qf.at[:, :, a].set(const).at[:, :, b].set(const)
        kf = kf.at[:, :, a].set(hi).at[:, :, b].set(lo)
    return qf.astype(BF16), kf.astype(BF16)


def _diff_attn_kernel(sl_ref, lam_ref, q_ref, k_ref, v_ref, qf_ref, kf_ref, g_ref, o_ref,
                      sa_ref, sb_ref, acc_ref, m_ref, vt_ref, *, tq, lam_init):
    h = pl.program_id(1)
    step = pl.program_id(2)
    slope2 = sl_ref[h]
    lam = lam_ref[0]

    @pl.when(step == 0)
    def _():
        vt_ref[0:DA_V_DIM, :] = jnp.transpose(v_ref[0].astype(F32)).astype(BF16)
        sub = lax.broadcasted_iota(I32, (DA_V_DIM, vt_ref.shape[1]), 0)
        vt_ref[DA_V_DIM:2 * DA_V_DIM, :] = jnp.where(sub == 0, 1.0, 0.0).astype(BF16)

    lane = lax.broadcasted_iota(I32, (tq, DA_V_DIM), 1)
    keeps = (lane < DA_HEAD_DIM, lane >= DA_HEAD_DIM)
    kf = kf_ref[0]
    row = lax.broadcasted_iota(I32, (tq, tq), 0)
    col = lax.broadcasted_iota(I32, (tq, tq), 1)
    blocks_per_step = q_ref.shape[1] // tq

    def query_block(sub):
        qi = step * blocks_per_step + sub
        rows = slice(sub * tq, (sub + 1) * tq)
        q = q_ref[0, rows, :]
        zero = jnp.zeros_like(q)
        q_maps = tuple(jnp.concatenate([jnp.where(keep, q, zero), qf_ref[0]], axis=1) for keep in keeps)

        def scores(j, s_ref):
            start = pl.multiple_of(j * tq, tq)
            kj = jnp.concatenate([k_ref[0, pl.ds(start, tq), :], kf], axis=1)
            for c in range(2):
                s_ref[c] = lax.dot_general(kj, q_maps[c], NT_DIMS, preferred_element_type=F32)

        def update(j, s_ref, diagonal):
            start = pl.multiple_of(j * tq, tq)
            vt = vt_ref[:, pl.ds(start, tq)]
            off = -slope2 * ((qi - j) * tq).astype(F32)
            for c in range(2):
                s = s_ref[c]
                if diagonal:
                    s = jnp.where(col >= row, s, NEG_BIG)
                m = m_ref[c]
                m_new = jnp.maximum(m, jnp.max(s, axis=0, keepdims=True) + off)
                p = jnp.exp2(s - (m_new - off)).astype(BF16)
                acc_ref[c] = jnp.exp2(m - m_new) * acc_ref[c] + jnp.dot(vt, p, preferred_element_type=F32)
                m_ref[c] = m_new

        m_ref[...] = jnp.full(m_ref.shape, NEG_BIG, F32)
        acc_ref[...] = jnp.zeros_like(acc_ref)

        scores(0, sa_ref)

        def pair(t, carry):
            b = 2 * t
            scores(b + 1, sb_ref)
            update(b, sa_ref, False)
            scores(b + 2, sa_ref)
            update(b + 1, sb_ref, False)
            return carry

        lax.fori_loop(0, qi // 2, pair, 0)

        @pl.when(qi % 2 == 1)
        def _():
            scores(qi, sb_ref)
            update(qi - 1, sa_ref, False)
            update(qi, sb_ref, True)

        @pl.when(qi % 2 == 0)
        def _():
            update(qi, sa_ref, True)

        acc_a = acc_ref[0]
        acc_b = acc_ref[1]
        o = (acc_a[:DA_V_DIM, :] / acc_a[DA_V_DIM:DA_V_DIM + 1, :]
             - lam * (acc_b[:DA_V_DIM, :] / acc_b[DA_V_DIM:DA_V_DIM + 1, :]))
        o = o * lax.rsqrt(jnp.mean(o * o, axis=0, keepdims=True) + LN_EPS) * g_ref[...]
        o_ref[0, rows, :] = (jnp.transpose(o) * (1.0 - lam_init)).astype(o_ref.dtype)

    for sub in range(blocks_per_step):
        query_block(sub)


def _diff_attn(h3, slopes, lam, subln_g, *, tq, lam_init):
    bsz, s, _ = h3.shape
    slopes2 = slopes * LOG2E
    qf, kf = _alibi_features(slopes2, tq)
    blk = DA_V_DIM
    q_blk0 = 0
    k_blk0 = q_blk0 + DA_HEADS
    v_blk0 = k_blk0 + DA_HEADS
    smem = pl.BlockSpec(memory_space=pltpu.SMEM)
    tstep = min(ATTN_BLOCKS_PER_STEP * tq, s)
    return pl.pallas_call(
        functools.partial(_diff_attn_kernel, tq=tq, lam_init=lam_init),
        grid=(bsz, DA_HEADS, s // tstep),
        in_specs=[smem, smem,
                  pl.BlockSpec((1, tstep, blk), lambda b, h, i: (b, i, q_blk0 + h)),
                  pl.BlockSpec((1, s, blk), lambda b, h, i: (b, 0, k_blk0 + h)),
                  pl.BlockSpec((1, s, blk), lambda b, h, i: (b, 0, v_blk0 + h)),
                  pl.BlockSpec((1, tq, blk), lambda b, h, i: (h, 0, 0)),
                  pl.BlockSpec((1, tq, blk), lambda b, h, i: (h, 0, 0)),
                  pl.BlockSpec((blk, 1), lambda b, h, i: (0, 0))],
        out_specs=pl.BlockSpec((1, tstep, blk), lambda b, h, i: (b, i, h)),
        out_shape=jax.ShapeDtypeStruct((bsz, s, DA_HEADS * DA_V_DIM), BF16),
        scratch_shapes=[pltpu.VMEM((2, tq, tq), F32), pltpu.VMEM((2, tq, tq), F32),
                        pltpu.VMEM((2, 2 * DA_V_DIM, tq), F32), pltpu.VMEM((2, 1, tq), F32),
                        pltpu.VMEM((2 * DA_V_DIM, s), BF16)],
        compiler_params=_cparams(("parallel", "parallel", "arbitrary")),
        name="diff_attn",
    )(slopes2, lam, h3, h3, h3, qf, kf, subln_g.reshape(blk, 1))


def _retention_kernel(q_ref, k_ref, v_ref, g_ref, dec_ref, qd_ref, kd_ref, gn_ref, o_ref, state_ref):
    tr = q_ref.shape[1]

    @pl.when(pl.program_id(1) == 0)
    def _():
        state_ref[...] = jnp.zeros_like(state_ref)

    for h in range(RET_HEADS):
        inner_decay = dec_ref[h]
        q_decay = qd_ref[h]
        k_decay = kd_ref[h]
        chunk_decay = math.exp(math.log1p(-(2.0 ** (-5.0 - h))) * RET_CHUNK)
        kc = slice(h * RET_DK, (h + 1) * RET_DK)
        vc = slice(h * RET_DV, (h + 1) * RET_DV)
        for c0 in range(0, tr, RET_CHUNK):
            rows = slice(c0, c0 + RET_CHUNK)
            q = q_ref[0, rows, kc]
            k = k_ref[0, rows, kc] * (RET_DK ** -0.5)
            v = v_ref[0, rows, vc]
            state = state_ref[h]
            scores = lax.dot_general(q, k, NT_DIMS, preferred_element_type=F32) * inner_decay
            inner = jnp.dot(scores.astype(BF16), v, preferred_element_type=F32)
            cross = jnp.dot(q, state.astype(BF16), preferred_element_type=F32) * q_decay
            k_dec = (k.astype(F32) * k_decay).astype(BF16)
            state_ref[h] = state * chunk_decay + lax.dot_general(
                k_dec, v, TN_DIMS, preferred_element_type=F32)
            o = inner + cross
            mu = jnp.mean(o, axis=-1, keepdims=True)
            oc = o - mu
            var = jnp.mean(oc * oc, axis=-1, keepdims=True)
            on = oc * lax.rsqrt(var + LN_EPS) * gn_ref[h:h + 1, :]
            gate = g_ref[0, rows, vc].astype(F32)
            o_ref[0, rows, vc] = (_silu(gate) * on).astype(o_ref.dtype)


def _retention(h3, gn_g, *, tr):
    bsz, s, _ = h3.shape
    dq = RET_HEADS * RET_DK
    dv = RET_HEADS * RET_DV
    log_g = jnp.log1p(-(2.0 ** (-5.0 - jnp.arange(RET_HEADS, dtype=F32))))[:, None, None]
    idx = jnp.arange(RET_CHUNK, dtype=F32)
    rel = idx[:, None] - idx[None, :]
    inner_decay = jnp.where(rel >= 0, jnp.exp(log_g * jnp.maximum(rel, 0.0)), 0.0)
    q_decay = jnp.exp(log_g * (idx[None, :, None] + 1.0))
    k_decay = jnp.exp(log_g * (RET_CHUNK - 1.0 - idx[None, :, None]))
    whole = lambda a: pl.BlockSpec(a.shape, lambda b, i: (0,) * a.ndim)
    return pl.pallas_call(
        _retention_kernel,
        grid=(bsz, s // tr),
        in_specs=[pl.BlockSpec((1, tr, dq), lambda b, i: (b, i, 0)),
                  pl.BlockSpec((1, tr, dq), lambda b, i: (b, i, 1)),
                  pl.BlockSpec((1, tr, dv), lambda b, i: (b, i, 1)),
                  pl.BlockSpec((1, tr, dv), lambda b, i: (b, i, 2)),
                  whole(inner_decay), whole(q_decay), whole(k_decay),
                  pl.BlockSpec(gn_g.shape, lambda b, i: (0, 0))],
        out_specs=pl.BlockSpec((1, tr, dv), lambda b, i: (b, i, 0)),
        out_shape=jax.ShapeDtypeStruct((bsz, s, dv), BF16),
        scratch_shapes=[pltpu.VMEM((RET_HEADS, RET_DK, RET_DV), F32)],
        compiler_params=_cparams(("parallel", "arbitrary")),
        name="retention",
    )(h3, h3, h3, h3, inner_decay, q_decay, k_decay, gn_g)


def _swiglu(xb, w1, w3, w2, tf):
    acc = None
    for f0 in range(0, w2.shape[0], tf):
        h1 = jnp.dot(xb, w1[:, f0:f0 + tf], preferred_element_type=F32)
        h3 = jnp.dot(xb, w3[:, f0:f0 + tf], preferred_element_type=F32)
        part = jnp.dot((_silu(h1) * h3).astype(BF16), w2[f0:f0 + tf, :], preferred_element_type=F32)
        acc = part if acc is None else acc + part
    return acc


def _ffn_kernel(x_ref, w1_ref, w3_ref, w2_ref, g_ref, b_ref, o_ref, *, tf, row_chunk):
    for r0 in range(0, x_ref.shape[0], row_chunk):
        rows = slice(r0, r0 + row_chunk)
        x = x_ref[rows, :]
        y = _swiglu(x.astype(BF16), w1_ref, w3_ref, w2_ref, tf)
        o_ref[rows, :] = _layer_norm(DN_ALPHA * x + y, g_ref[...], b_ref[...])


def _ffn(x, w1, w3, w2, g, b, *, tm, tf):
    m, d = x.shape
    resident = dict(pipeline_mode=pl.Buffered(1))
    return pl.pallas_call(
        functools.partial(_ffn_kernel, tf=tf, row_chunk=min(512, tm)),
        grid=(m // tm,),
        in_specs=[pl.BlockSpec((tm, d), lambda i: (i, 0)),
                  pl.BlockSpec(w1.shape, lambda i: (0, 0), **resident),
                  pl.BlockSpec(w3.shape, lambda i: (0, 0), **resident),
                  pl.BlockSpec(w2.shape, lambda i: (0, 0), **resident),
                  pl.BlockSpec((1, d), lambda i: (0, 0)),
                  pl.BlockSpec((1, d), lambda i: (0, 0))],
        out_specs=pl.BlockSpec((tm, d), lambda i: (i, 0)),
        out_shape=jax.ShapeDtypeStruct((m, d), F32),
        compiler_params=_cparams(("parallel",)),
        name="ffn",
    )(x, w1, w3, w2, g.reshape(1, d), b.reshape(1, d))


def _experts_kernel(te_ref, nv_ref, xs_ref, w1_ref, w3_ref, w2_ref, ys_ref, *, tf):
    del te_ref
    i = pl.program_id(0)

    @pl.when(i < nv_ref[0])
    def _():
        ys_ref[...] = _swiglu(xs_ref[...].astype(BF16), w1_ref.at[0, 0], w3_ref.at[0, 0], w2_ref.at[0, 0], tf)

    @pl.when(i >= nv_ref[0])
    def _():
        ys_ref[...] = jnp.zeros_like(ys_ref)


def _experts(xs, tile_expert, n_valid, w1, w3, w2, layer, *, tm, tf):
    r, d = xs.shape
    resident = dict(pipeline_mode=pl.Buffered(1))

    def expert_block(w):
        return pl.BlockSpec((1, 1) + w.shape[2:], lambda i, te, nv: (layer, te[i], 0, 0), **resident)

    grid_spec = pltpu.PrefetchScalarGridSpec(
        num_scalar_prefetch=2,
        grid=(r // tm,),
        in_specs=[pl.BlockSpec((tm, d), lambda i, te, nv: (jnp.minimum(i, nv[0] - 1), 0)),
                  expert_block(w1), expert_block(w3), expert_block(w2)],
        out_specs=pl.BlockSpec((tm, d), lambda i, te, nv: (i, 0)),
    )
    return pl.pallas_call(
        functools.partial(_experts_kernel, tf=tf),
        grid_spec=grid_spec,
        out_shape=jax.ShapeDtypeStruct((r, d), F32),
        compiler_params=_cparams(("arbitrary",)),
        name="experts",
    )(tile_expert, n_valid, xs, w1, w3, w2)


def _split_bf16(x):
    hi = x.astype(BF16)
    return hi, (x - hi.astype(F32)).astype(BF16)


def _router_kernel(x_ref, wt_ref, b_ref, meta_ref, gate_ref, cnt_ref, carry_ref, tri_ref):
    i = pl.program_id(0)
    tm = x_ref.shape[0]

    @pl.when(i == 0)
    def _():
        carry_ref[...] = jnp.zeros_like(carry_ref)
        a = lax.broadcasted_iota(I32, (tm, tm), 0)
        b = lax.broadcasted_iota(I32, (tm, tm), 1)
        tri_ref[...] = (a <= b).astype(BF16)

    xh, xl = _split_bf16(x_ref[...])
    wh, wl = _split_bf16(wt_ref[...])
    logits = (lax.dot_general(wh, xh, NT_DIMS, preferred_element_type=F32)
              + lax.dot_general(wh, xl, NT_DIMS, preferred_element_type=F32)
              + lax.dot_general(wl, xh, NT_DIMS, preferred_element_type=F32)
              + b_ref[...])

    e_iota = lax.broadcasted_iota(I32, logits.shape, 0)
    m1 = jnp.max(logits, axis=0, keepdims=True)
    i1 = jnp.min(jnp.where(logits == m1, e_iota, N_EXPERTS), axis=0, keepdims=True)
    rest = jnp.where(e_iota == i1, -jnp.inf, logits)
    m2 = jnp.max(rest, axis=0, keepdims=True)
    i2 = jnp.min(jnp.where(rest == m2, e_iota, N_EXPERTS), axis=0, keepdims=True)
    e2 = jnp.exp(m2 - m1)
    gate_ref[0:1, :] = 1.0 / (1.0 + e2)
    gate_ref[1:2, :] = e2 / (1.0 + e2)

    pick1 = e_iota == i1
    pick2 = e_iota == i2
    chosen = jnp.where(pick1 | pick2, 1.0, 0.0)
    incl = jnp.dot(chosen.astype(BF16), tri_ref[...], preferred_element_type=F32)
    before = carry_ref[...] + incl - chosen
    meta_ref[0:1, :] = i1
    meta_ref[1:2, :] = i2
    meta_ref[2:3, :] = jnp.sum(jnp.where(pick1, before, 0.0), axis=0, keepdims=True).astype(I32)
    meta_ref[3:4, :] = jnp.sum(jnp.where(pick2, before, 0.0), axis=0, keepdims=True).astype(I32)
    carry_ref[...] += incl[:, tm - 1:tm]
    cnt_ref[...] = jnp.broadcast_to(carry_ref[...], cnt_ref.shape).astype(I32)


def _router(x, w_router, b_router, *, tm):
    m, d = x.shape
    return pl.pallas_call(
        _router_kernel,
        grid=(m // tm,),
        in_specs=[pl.BlockSpec((tm, d), lambda i: (i, 0)),
                  pl.BlockSpec((N_EXPERTS, d), lambda i: (0, 0)),
                  pl.BlockSpec((N_EXPERTS, 1), lambda i: (0, 0))],
        out_specs=[pl.BlockSpec((4, tm), lambda i: (0, i)),
                   pl.BlockSpec((TOP_K, tm), lambda i: (0, i)),
                   pl.BlockSpec((N_EXPERTS, 128), lambda i: (0, 0))],
        out_shape=[jax.ShapeDtypeStruct((4, m), I32),
                   jax.ShapeDtypeStruct((TOP_K, m), F32),
                   jax.ShapeDtypeStruct((N_EXPERTS, 128), I32)],
        scratch_shapes=[pltpu.VMEM((N_EXPERTS, 1), F32), pltpu.VMEM((tm, tm), BF16)],
        compiler_params=_cparams(("arbitrary",)),
        name="router",
    )(x, w_router.T, b_router.reshape(N_EXPERTS, 1))


ROW_DMA_UNROLL = 8


def _for_each_row(tt, fn):
    def trip(t, carry):
        for k in range(ROW_DMA_UNROLL):
            fn(t * ROW_DMA_UNROLL + k, k)
        return carry

    lax.fori_loop(0, tt // ROW_DMA_UNROLL, trip, 0)


def _dispatch_kernel(dest_ref, dest_prev_ref, pad_ref, x_ref, xs_ref, stage_ref, zero_ref, sem, zero_sem, *, tt):
    i = pl.program_id(0)
    slot = i % 2

    def zero_copy(e):
        start = pl.multiple_of(pad_ref[e], SUBLANES)
        return pltpu.make_async_copy(zero_ref, xs_ref.at[pl.ds(start, zero_ref.shape[0]), :], zero_sem)

    te = zero_ref.shape[0] - SUBLANES
    n_tiles = xs_ref.shape[0] // te

    def tail_copy(t):
        return pltpu.make_async_copy(zero_ref.at[pl.ds(0, te), :],
                                     xs_ref.at[pl.ds(pl.multiple_of(t * te, te), te), :], zero_sem)

    @pl.when(i == 0)
    def _():
        zero_ref[...] = jnp.zeros_like(zero_ref)
        for e in range(N_EXPERTS):
            zero_copy(e).start()
            zero_copy(e).wait()
        for e in range(N_EXPERTS):
            t = pad_ref[N_EXPERTS] + e

            @pl.when(t < n_tiles)
            def _():
                tail_copy(t).start()
                tail_copy(t).wait()

    def copy(idx_ref, r, j, s):
        return pltpu.make_async_copy(stage_ref.at[s, pl.ds(r, 1), :],
                                     xs_ref.at[pl.ds(idx_ref[0, 0, j * tt + r], 1), :], sem.at[s])

    def wait_all(idx_ref, s):
        _for_each_row(tt, lambda r, k: [copy(idx_ref, r, j, s).wait() for j in range(TOP_K)])

    stage_ref[slot] = x_ref[...]
    _for_each_row(tt, lambda r, k: [copy(dest_ref, r, j, slot).start(priority=(k + j) % 2) for j in range(TOP_K)])

    @pl.when(i > 0)
    def _():
        wait_all(dest_prev_ref, 1 - slot)

    @pl.when(i == pl.num_programs(0) - 1)
    def _():
        wait_all(dest_ref, slot)


def _dispatch(x, dest_tiles, pad_start, rows, *, tt, te):
    m, d = x.shape
    idx_block = (1, 1, TOP_K * tt)
    return pl.pallas_call(
        functools.partial(_dispatch_kernel, tt=tt),
        grid=(m // tt,),
        in_specs=[pl.BlockSpec(idx_block, lambda i: (i, 0, 0), memory_space=pltpu.SMEM),
                  pl.BlockSpec(idx_block, lambda i: (jnp.maximum(i - 1, 0), 0, 0), memory_space=pltpu.SMEM),
                  pl.BlockSpec(memory_space=pltpu.SMEM),
                  pl.BlockSpec((tt, d), lambda i: (i, 0))],
        out_specs=pl.BlockSpec(memory_space=pl.ANY),
        out_shape=jax.ShapeDtypeStruct((rows, d), F32),
        scratch_shapes=[pltpu.VMEM((2, tt, d), F32), pltpu.VMEM((te + SUBLANES, d), F32),
                        pltpu.SemaphoreType.DMA((2,)), pltpu.SemaphoreType.DMA(())],
        compiler_params=_cparams(("arbitrary",)),
        name="dispatch",
    )(dest_tiles, dest_tiles, pad_start, x)


def _combine_kernel(dest_ref, dest_next_ref, x_ref, gate_ref, g_ref, b_ref, ys_ref, o_ref, buf_ref, sem, *, tt):
    i = pl.program_id(0)
    slot = i % 2

    def copy(idx_ref, r, j, s):
        return pltpu.make_async_copy(ys_ref.at[pl.ds(idx_ref[0, 0, j * tt + r], 1), :],
                                     buf_ref.at[s, j, pl.ds(r, 1), :], sem.at[s])

    def start_all(idx_ref, s):
        _for_each_row(tt, lambda r, k: [copy(idx_ref, r, j, s).start(priority=(k + j) % 2) for j in range(TOP_K)])

    @pl.when(i == 0)
    def _():
        start_all(dest_ref, slot)

    @pl.when(i + 1 < pl.num_programs(0))
    def _():
        start_all(dest_next_ref, 1 - slot)

    _for_each_row(tt, lambda r, k: [copy(dest_ref, r, j, slot).wait() for j in range(TOP_K)])

    y = gate_ref[:, 0:1] * buf_ref[slot, 0] + gate_ref[:, 1:2] * buf_ref[slot, 1]
    o_ref[...] = _layer_norm(DN_ALPHA * x_ref[...] + y, g_ref[...], b_ref[...])


def _combine(x, ys, dest_tiles, gates_t, g, b, *, tt):
    m, d = x.shape
    n = m // tt
    idx_block = (1, 1, TOP_K * tt)
    return pl.pallas_call(
        functools.partial(_combine_kernel, tt=tt),
        grid=(n,),
        in_specs=[pl.BlockSpec(idx_block, lambda i: (i, 0, 0), memory_space=pltpu.SMEM),
                  pl.BlockSpec(idx_block, lambda i: (jnp.minimum(i + 1, n - 1), 0, 0), memory_space=pltpu.SMEM),
                  pl.BlockSpec((tt, d), lambda i: (i, 0)),
                  pl.BlockSpec((tt, TOP_K), lambda i: (i, 0)),
                  pl.BlockSpec((1, d), lambda i: (0, 0)),
                  pl.BlockSpec((1, d), lambda i: (0, 0)),
                  pl.BlockSpec(memory_space=pl.ANY)],
        out_specs=pl.BlockSpec((tt, d), lambda i: (i, 0)),
        out_shape=jax.ShapeDtypeStruct((m, d), F32),
        scratch_shapes=[pltpu.VMEM((2, TOP_K, tt, d), F32), pltpu.SemaphoreType.DMA((2,))],
        compiler_params=_cparams(("arbitrary",)),
        name="combine",
    )(dest_tiles, dest_tiles, x, gates_t, g.reshape(1, d), b.reshape(1, d), ys)


def _moe(x, w_router, b_router, w1, w3, w2, layer, g, b, *, tile):
    m, d = x.shape
    meta, gates, counts = _router(x, w_router, b_router, tm=tile.router)
    counts = counts[:, 0]
    te = tile.expert
    padded = (counts + te - 1) // te * te
    ends = jnp.cumsum(padded)
    starts = ends - padded
    expert_ids = meta[0:TOP_K]
    first_row = sum(jnp.where(expert_ids == e, starts[e], 0) for e in range(N_EXPERTS))
    dest = first_row + meta[TOP_K:2 * TOP_K]
    n_tiles = TOP_K * m // te + N_EXPERTS
    tile_start = jnp.arange(n_tiles, dtype=I32) * te
    tile_expert = jnp.minimum(jnp.sum((tile_start[:, None] >= ends[None, :]).astype(I32), axis=1), N_EXPERTS - 1)
    n_valid = (ends[-1:] // te).astype(I32)

    rows = n_tiles * te
    pad_start = jnp.minimum((starts + counts) // SUBLANES * SUBLANES, rows - te - SUBLANES).astype(I32)
    pad_start = jnp.concatenate([pad_start, n_valid])

    tt = tile.token
    dest_tiles = dest.reshape(TOP_K, m // tt, tt).transpose(1, 0, 2).reshape(m // tt, 1, TOP_K * tt)
    xs = _dispatch(x, dest_tiles, pad_start, rows, tt=tt, te=te)
    ys = _experts(xs, tile_expert, n_valid, w1, w3, w2, layer, tm=te, tf=tile.ff)
    return _combine(x, ys, dest_tiles, gates.T, g, b, tt=tt)


class _Tiles:
    def __init__(self, bsz, seq):
        m = bsz * seq
        self.proj = min(1024, m)
        self.attn = min(512, seq)
        self.ret = min(512, seq)
        self.ffn = min(1024, m)
        self.ff = 512
        self.router = min(1024, m)
        self.expert = min(1024, m)
        self.token = min(512, m)


def _even_layer(x, p, layer_idx, bsz, seq, tile):
    m, d = x.shape
    a_out, h = _in_proj_sgu(x, p["w_in"], p["w_s"], p["b_s"], p["v_ln_g"], p["v_ln_b"],
                            tm=tile.proj, q_scale=DA_HEAD_DIM ** -0.5 * LOG2E)
    lam_init = 0.8 - 0.6 * math.exp(-0.3 * layer_idx)
    lam = (jnp.exp(jnp.sum(p["lam_q1"] * p["lam_k1"])) - jnp.exp(jnp.sum(p["lam_q2"] * p["lam_k2"]))
           + lam_init).reshape(1).astype(F32)
    slopes = 2.0 ** (-8.0 * jnp.arange(1, DA_HEADS + 1, dtype=F32) / DA_HEADS)
    b_out = _diff_attn(h.reshape(bsz, seq, -1), slopes, lam, p["subln_g"], tq=tile.attn, lam_init=lam_init)
    x = _proj_res_ln([a_out, b_out.reshape(m, -1)], p["w_o"], x, p["ln1_g"], p["ln1_b"], tm=tile.proj)
    return _ffn(x, p["ffn_w1"], p["ffn_w3"], p["ffn_w2"], p["ln2_g"], p["ln2_b"], tm=tile.ffn, tf=tile.ff)


def _odd_layer(x, p, bsz, seq, tile):
    m, d = x.shape
    n_in = p["w_in"].shape[1]
    h = _matmul(x, p["w_in"], tm=tile.proj, tn=n_in // 2)
    o = _retention(h.reshape(bsz, seq, n_in), p["gn_g"], tr=tile.ret)
    x = _proj_res_ln([o.reshape(m, -1)], p["w_o"], x, p["ln1_g"], p["ln1_b"], tm=tile.proj)
    return _moe(x, p["router_w"], p["router_b"], p["moe_w1"], p["moe_w3"], p["moe_w2"], p["moe_layer"],
                p["ln2_g"], p["ln2_b"], tile=tile)


def kernel(x, even_w_in, even_w_s, even_b_s, even_v_ln_g, even_v_ln_b, even_lam_q1, even_lam_k1, even_lam_q2, even_lam_k2, even_subln_g, even_w_o, even_ln1_g, even_ln1_b, ffn_w1, ffn_w3, ffn_w2, even_ln2_g, even_ln2_b, odd_w_in, odd_gn_g, odd_w_o, odd_ln1_g, odd_ln1_b, router_w, router_b, moe_w1, moe_w3, moe_w2, odd_ln2_g, odd_ln2_b):
    bsz, seq, d = x.shape
    tile = _Tiles(bsz, seq)
    bf = lambda w: w.astype(BF16)
    moe_w1_bf, moe_w3_bf, moe_w2_bf = bf(moe_w1), bf(moe_w3), bf(moe_w2)
    xf = x.reshape(bsz * seq, d)
    for layer in range(DEPTH):
        i = layer // 2
        if layer % 2 == 0:
            p = dict(w_in=bf(even_w_in[i]), w_s=even_w_s[i], b_s=even_b_s[i], v_ln_g=even_v_ln_g[i],
                     v_ln_b=even_v_ln_b[i], lam_q1=even_lam_q1[i], lam_k1=even_lam_k1[i],
                     lam_q2=even_lam_q2[i], lam_k2=even_lam_k2[i], subln_g=even_subln_g[i],
                     w_o=bf(even_w_o[i]), ln1_g=even_ln1_g[i], ln1_b=even_ln1_b[i],
                     ffn_w1=bf(ffn_w1[i]), ffn_w3=bf(ffn_w3[i]), ffn_w2=bf(ffn_w2[i]),
                     ln2_g=even_ln2_g[i], ln2_b=even_ln2_b[i])
            xf = _even_layer(xf, p, layer, bsz, seq, tile)
        else:
            p = dict(w_in=bf(odd_w_in[i]), gn_g=odd_gn_g[i], w_o=bf(odd_w_o[i]),
                     ln1_g=odd_ln1_g[i], ln1_b=odd_ln1_b[i], router_w=router_w[i], router_b=router_b[i],
                     moe_w1=moe_w1_bf, moe_w3=moe_w3_bf, moe_w2=moe_w2_bf, moe_layer=i,
                     ln2_g=odd_ln2_g[i], ln2_b=odd_ln2_b[i])
            xf = _odd_layer(xf, p, bsz, seq, tile)
    return xf.reshape(bsz, seq, d)
```
